```python
import math
import jax, jax.numpy as jnp
from jax import lax
import numpy as np

D_MODEL = 1024
BATCH = 4
SEQ = 8192
DEPTH = 1
DEC_BATCH = 128
DEC_SEQ = 8
PAST_LEN = 16384
PAGE_SIZE = 128

NSA_HEADS = 8
NSA_KV_GROUPS = 2
NSA_HPG = NSA_HEADS // NSA_KV_GROUPS
NSA_DH = 64
CMP_BLOCK = 32
CMP_STRIDE = 16
CMP_HIDDEN = 64
SEL_BLOCK = 64
SEL_TOPK = 16
WINDOW = 512
FORCE_BONUS = 1000.0
MLA_HEADS = 8
MLA_Q_RANK = 256
MLA_KV_RANK = 128
MLA_NOPE = 64
MLA_ROPE = 32
MLA_V = 64
ROPE_THETA = 10000.0
N_GROUPS = 4
EXPERTS_PER_GROUP = 8
N_EXPERTS = N_GROUPS * EXPERTS_PER_GROUP
TOP_K_IN_GROUP = 2
D_EXPERT = 256
MOE_BLOCK = 128
PLE_DIM = 256
Q_BLOCK = 128
DEEPNORM_ALPHA = (2.0 * DEPTH) ** 0.25
DEEPNORM_BETA = (8.0 * DEPTH) ** -0.25
LN_EPS = 1e-5
RMS_EPS = 1e-6
KVW = NSA_KV_GROUPS * NSA_DH
IN_SIZES = (NSA_HEADS * NSA_DH, KVW, KVW, KVW, KVW, KVW, KVW, 3 * NSA_HEADS,
            MLA_Q_RANK, MLA_KV_RANK, MLA_ROPE, D_MODEL, D_MODEL)
N_IN = NSA_HEADS * NSA_DH + 6 * KVW + 3 * NSA_HEADS + MLA_Q_RANK + MLA_KV_RANK + MLA_ROPE + 2 * D_MODEL

kernel_name = "nsa_mla_gated_hmoe_decoder_step"


def _layernorm(x, g, b):
    xf = x.astype(jnp.float32)
    mu = jnp.mean(xf, -1, keepdims=True)
    var = jnp.mean(jnp.square(xf - mu), -1, keepdims=True)
    return ((xf - mu) * lax.rsqrt(var + LN_EPS) * g + b).astype(x.dtype)


def _rmsnorm(x, g):
    xf = x.astype(jnp.float32)
    return (xf * lax.rsqrt(jnp.mean(xf * xf, -1, keepdims=True) + RMS_EPS) * g).astype(x.dtype)


def _rope(x, pos):
    half = x.shape[-1] // 2
    inv = ROPE_THETA ** (-jnp.arange(half, dtype=jnp.float32) / half)
    ang = pos.astype(jnp.float32)[..., None] * inv
    cos, sin = jnp.cos(ang).astype(x.dtype), jnp.sin(ang).astype(x.dtype)
    x1, x2 = x[..., :half], x[..., half:]
    return jnp.concatenate([x1 * cos - x2 * sin, x1 * sin + x2 * cos], -1)


def _masked_softmax(s, mask):
    s = jnp.where(mask, s.astype(jnp.float32), -jnp.inf)
    m = jnp.max(s, axis=-1, keepdims=True)
    m = jnp.where(jnp.isfinite(m), m, 0.0)
    e = jnp.where(mask, jnp.exp(s - m), 0.0)
    den = jnp.sum(e, axis=-1, keepdims=True)
    return e / jnp.where(den > 0, den, 1.0)


def _alibi_slopes():
    h = jnp.arange(1, NSA_HEADS + 1, dtype=jnp.float32)
    return jnp.exp2(-8.0 * h / NSA_HEADS).reshape(NSA_KV_GROUPS, NSA_HPG)


def _overlap(nc, ns):
    c0 = np.arange(nc)[:, None] * CMP_STRIDE
    c1 = c0 + CMP_BLOCK - 1
    s0 = np.arange(ns)[None, :] * SEL_BLOCK
    s1 = s0 + SEL_BLOCK - 1
    return jnp.asarray(((c0 <= s1) & (c1 >= s0)).astype(np.float32))


def _compress(k, pe, w1, w2):
    b, t, g, d = k.shape
    nc = (t - CMP_BLOCK) // CMP_STRIDE + 1
    idx = np.arange(nc)[:, None] * CMP_STRIDE + np.arange(CMP_BLOCK)[None, :]
    blk = k[:, idx] + pe[None, None, :, None, :]
    blk = jnp.transpose(blk, (0, 1, 3, 2, 4)).reshape(b, nc, g, CMP_BLOCK * d)
    return jax.nn.gelu(blk @ w1) @ w2


def _sel_blocks(k):
    b, t, g, d = k.shape
    ns = -(-t // SEL_BLOCK)
    k = jnp.pad(k, ((0, 0), (0, ns * SEL_BLOCK - t), (0, 0), (0, 0)))
    return k.reshape(b, ns, SEL_BLOCK, g, d).transpose(0, 3, 1, 2, 4)


def _nsa_block(q, q_pos, gates, kc, vc, c_end, kb, vb, kw, vw, w_pos):
    f32 = jnp.float32
    scale = NSA_DH ** -0.5
    sl = _alibi_slopes()[None, :, :, None, None]
    b = q.shape[0]
    dist_c = q_pos[:, None] - c_end[None, :]
    s_c = jnp.einsum('bqghd,bcgd->bghqc', q, kc).astype(f32) * scale - sl * dist_c.astype(f32)
    p_c = _masked_softmax(s_c, dist_c >= 0)
    o_c = jnp.einsum('bghqc,bcgd->bqghd', p_c.astype(vc.dtype), vc)
    ns = kb.shape[2]
    imp = jnp.einsum('bghqc,cs->bgqs', p_c, _overlap(kc.shape[1], ns))
    blk = jnp.arange(ns)
    cur = q_pos // SEL_BLOCK
    avail = (blk * SEL_BLOCK)[None, :] <= q_pos[:, None]
    forced = (blk[None, :] == 0) | (blk[None, :] == cur[:, None]) | (blk[None, :] == cur[:, None] - 1)
    score = jnp.where(avail, imp + FORCE_BONUS * forced.astype(f32), -jnp.inf)
    top_val, top_idx = lax.top_k(score, min(SEL_TOPK, ns))
    ok = jnp.isfinite(top_val)
    bi = jnp.arange(b)[:, None, None, None]
    gi = jnp.arange(NSA_KV_GROUPS)[None, :, None, None]
    ks = kb[bi, gi, top_idx]
    vs = vb[bi, gi, top_idx]
    pos_s = top_idx[..., None] * SEL_BLOCK + jnp.arange(SEL_BLOCK)
    dist_s = (q_pos[None, None, :, None, None] - pos_s)[:, :, None]
    mask_s = ok[:, :, None, :, :, None] & (dist_s >= 0)
    s_s = jnp.einsum('bqghd,bgqnld->bghqnl', q, ks).astype(f32) * scale - sl[..., None] * dist_s.astype(f32)
    shp = s_s.shape
    p_s = _masked_softmax(s_s.reshape(shp[:4] + (-1,)),
                          jnp.broadcast_to(mask_s, shp).reshape(shp[:4] + (-1,))).reshape(shp)
    o_s = jnp.einsum('bghqnl,bgqnld->bqghd', p_s.astype(vs.dtype), vs)
    dist_w = q_pos[:, None] - w_pos[None, :]
    mask_w = (dist_w >= 0) & (dist_w < WINDOW) & (w_pos[None, :] >= 0)
    s_w = jnp.einsum('bqghd,bkgd->bghqk', q, kw).astype(f32) * scale - sl * dist_w.astype(f32)
    p_w = _masked_softmax(s_w, mask_w)
    o_w = jnp.einsum('bghqk,bkgd->bqghd', p_w.astype(vw.dtype), vw)
    g = jax.nn.sigmoid(gates)
    return g[..., 0:1] * o_c + g[..., 1:2] * o_s + g[..., 2:3] * o_w


def _mla_attend(q_lat, q_pe, q_pos, ckv, kpe, k_pos, w_uv):
    s = (jnp.einsum('bqhc,bkc->bhqk', q_lat, ckv) + jnp.einsum('bqhr,bkr->bhqk', q_pe, kpe)).astype(jnp.float32)
    s = s * (MLA_NOPE + MLA_ROPE) ** -0.5
    p = _masked_softmax(s, k_pos[None, :] <= q_pos[:, None])
    o_lat = jnp.einsum('bhqk,bkc->bqhc', p.astype(ckv.dtype), ckv)
    return jnp.einsum('bqhc,chv->bqhv', o_lat, w_uv)


def _mixer_inputs(x, pos, w_in, q_norm, kv_norm, w_uq, w_uk):
    b, t, _ = x.shape
    z = x @ w_in
    offs = [int(o) for o in np.cumsum(IN_SIZES)[:-1]]
    q, kc, vc, ks, vs, kw, vw, gt, dq, dkv, kr, ga, gb = jnp.split(z, offs, axis=-1)
    kvs = (b, t, NSA_KV_GROUPS, NSA_DH)
    q_nsa = q.reshape(b, t, NSA_KV_GROUPS, NSA_HPG, NSA_DH)
    nsa_rows = jnp.stack([a.reshape(kvs) for a in (kc, vc, ks, vs)], axis=2)
    win_rows = jnp.stack([kw.reshape(kvs), vw.reshape(kvs)], axis=2)
    gates = gt.reshape(b, t, NSA_KV_GROUPS, NSA_HPG, 3)
    qm = (_rmsnorm(dq, q_norm) @ w_uq).reshape(b, t, MLA_HEADS, MLA_NOPE + MLA_ROPE)
    q_pe = _rope(qm[..., MLA_NOPE:], pos[None, :, None])
    q_lat = jnp.einsum('bthn,chn->bthc', qm[..., :MLA_NOPE], w_uk)
    mla_rows = jnp.concatenate([_rmsnorm(dkv, kv_norm), _rope(kr, pos[None, :])], -1)
    return q_nsa, gates, nsa_rows, win_rows, q_lat, q_pe, mla_rows, ga, gb


def _prompt_attention(q_nsa, gates, nsa_rows, win_rows, q_lat, q_pe, mla_rows, w_uv, cmp):
    b, s = q_nsa.shape[:2]
    pe_k, pe_v, k_w1, k_w2, v_w1, v_w2 = cmp
    kc = _compress(nsa_rows[:, :, 0], pe_k, k_w1, k_w2)
    vc = _compress(nsa_rows[:, :, 1], pe_v, v_w1, v_w2)
    c_end = jnp.arange(kc.shape[1]) * CMP_STRIDE + CMP_BLOCK - 1
    kb = _sel_blocks(nsa_rows[:, :, 2])
    vb = _sel_blocks(nsa_rows[:, :, 3])
    win_pad = jnp.pad(win_rows, ((0, 0), (WINDOW, 0), (0, 0), (0, 0), (0, 0)))
    ckv = mla_rows[..., :MLA_KV_RANK]
    kpe = mla_rows[..., MLA_KV_RANK:]
    k_pos = jnp.arange(s)

    def block(i):
        s0 = i * Q_BLOCK
        q_pos = s0 + jnp.arange(Q_BLOCK)
        sl = lambda a: lax.dynamic_slice_in_dim(a, s0, Q_BLOCK, axis=1)
        win = lax.dynamic_slice_in_dim(win_pad, s0, WINDOW + Q_BLOCK, axis=1)
        w_pos = s0 - WINDOW + jnp.arange(WINDOW + Q_BLOCK)
        o_n = _nsa_block(sl(q_nsa), q_pos, sl(gates), kc, vc, c_end, kb, vb, win[:, :, 0], win[:, :, 1], w_pos)
        o_m = _mla_attend(sl(q_lat), sl(q_pe), q_pos, ckv, kpe, k_pos, w_uv)
        return o_n, o_m

    o_n, o_m = lax.map(block, jnp.arange(s // Q_BLOCK))
    o_n = jnp.moveaxis(o_n, 0, 1).reshape(b, s, NSA_HEADS * NSA_DH)
    o_m = jnp.moveaxis(o_m, 0, 1).reshape(b, s, MLA_HEADS * MLA_V)
    return o_n, o_m


def _sample_attention(q_nsa, gates, nsa_new, win_new, q_lat, q_pe, mla_new, pool_nsa, pool_mla, win_buf, page_table, w_uv, cmp):
    pe_k, pe_v, k_w1, k_w2, v_w1, v_w2 = cmp
    db, dec = q_nsa.shape[:2]
    past = page_table.shape[1] * pool_nsa.shape[1]
    q_pos = past + jnp.arange(dec)
    k_pos = jnp.arange(past + dec)
    wb = win_buf.shape[1]
    w_pos = past - wb + jnp.arange(wb + dec)
    nc = (past + dec - CMP_BLOCK) // CMP_STRIDE + 1
    c_end = jnp.arange(nc) * CMP_STRIDE + CMP_BLOCK - 1

    def one(args):
        qn, gt, nn, wn, ql, qp, mn, wbuf, pt = args
        rows = jnp.concatenate([pool_nsa[pt].reshape((past,) + nn.shape[1:]), nn], 0)[None]
        kc = _compress(rows[:, :, 0], pe_k, k_w1, k_w2)
        vc = _compress(rows[:, :, 1], pe_v, v_w1, v_w2)
        kb = _sel_blocks(rows[:, :, 2])
        vb = _sel_blocks(rows[:, :, 3])
        win = jnp.concatenate([wbuf, wn], 0)[None]
        o_n = _nsa_block(qn[None], q_pos, gt[None], kc, vc, c_end, kb, vb, win[:, :, 0], win[:, :, 1], w_pos)[0]
        mrows = jnp.concatenate([pool_mla[pt].reshape(past, -1), mn], 0)[None]
        o_m = _mla_attend(ql[None], qp[None], q_pos, mrows[..., :MLA_KV_RANK], mrows[..., MLA_KV_RANK:], k_pos, w_uv)[0]
        return o_n, o_m

    o_n, o_m = lax.map(one, (q_nsa, gates, nsa_new, win_new, q_lat, q_pe, mla_new, win_buf, page_table))
    return o_n.reshape(db, dec, NSA_HEADS * NSA_DH), o_m.reshape(db, dec, MLA_HEADS * MLA_V)


def _moe(x, w_group, b_group, w_router, b_router, w_gate, w_up, w_down):
    shp = x.shape
    xt = x.reshape(-1, shp[-1])
    n = xt.shape[0]
    g_logit = (xt @ w_group + b_group).astype(jnp.float32)
    g_sel = jnp.argmax(g_logit, axis=-1)
    g_w = jnp.take_along_axis(jax.nn.softmax(g_logit, -1), g_sel[:, None], -1)
    e_logit = (xt @ w_router + b_router).astype(jnp.float32).reshape(n, N_GROUPS, EXPERTS_PER_GROUP)
    e_logit = jnp.take_along_axis(e_logit, g_sel[:, None, None], axis=1)[:, 0]
    top_val, top_idx = lax.top_k(e_logit, TOP_K_IN_GROUP)
    gate = (g_w * jax.nn.softmax(top_val, -1)).reshape(-1)
    eid = (g_sel[:, None] * EXPERTS_PER_GROUP + top_idx).reshape(-1)
    m = eid.shape[0]
    order = jnp.argsort(eid)
    e_sorted = eid[order]
    tok = order // TOP_K_IN_GROUP
    counts = jnp.bincount(eid, length=N_EXPERTS)
    padded = (counts + MOE_BLOCK - 1) // MOE_BLOCK * MOE_BLOCK
    pad_end = jnp.cumsum(padded)
    start = jnp.cumsum(counts) - counts
    dest = (pad_end - padded)[e_sorted] + jnp.arange(m) - start[e_sorted]
    nb = -(-m // MOE_BLOCK) + N_EXPERTS
    buf = jnp.zeros((nb * MOE_BLOCK, shp[-1]), x.dtype).at[dest].set(xt[tok])
    blk_e = jnp.minimum(jnp.sum(pad_end[None, :] <= (jnp.arange(nb) * MOE_BLOCK)[:, None], axis=1), N_EXPERTS - 1)

    def run(args):
        xb, e = args
        hb = jax.nn.silu(xb @ w_gate[e]) * (xb @ w_up[e])
        return hb @ w_down[e]

    yb = lax.map(run, (buf.reshape(nb, MOE_BLOCK, -1), blk_e)).reshape(nb * MOE_BLOCK, -1)
    ys = yb[dest] * gate[order][:, None].astype(x.dtype)
    return jnp.zeros_like(xt).at[tok].add(ys).reshape(shp)


def _layer_tail(h, o_n, o_m, ga, gb, p, w_up_a, w_up_b, w_o, ln1_g, ln1_b, moe_w, ln2_g, ln2_b, w_pe, w_pg):
    mix = (jax.nn.sigmoid(ga) * (o_n @ w_up_a) + jax.nn.sigmoid(gb) * (o_m @ w_up_b)) @ w_o
    h = _layernorm(DEEPNORM_ALPHA * h + mix, ln1_g, ln1_b)
    h = _layernorm(DEEPNORM_ALPHA * h + _moe(h, *moe_w), ln2_g, ln2_b)
    return h + jax.nn.sigmoid(h @ w_pg) * (p @ w_pe)


def setup_inputs(seed: int = 0) -> dict:
    key = jax.random.key(seed)
    ks = iter(jax.random.split(key, 48))

    def nrm(shape, scale=1.0):
        return jax.random.normal(next(ks), shape, jnp.float32) * scale

    n_pages = PAST_LEN // PAGE_SIZE
    n_used = DEC_BATCH * n_pages
    n_pool = n_used + max(1, n_used // 4)
    w_buf = min(WINDOW, PAST_LEN)
    beta = DEEPNORM_BETA
    col_scale = jnp.concatenate([jnp.full((sz,), beta if i in (2, 4, 6) else 1.0, jnp.float32)
                                 for i, sz in enumerate(IN_SIZES)])
    out = {}
    out['x_prompt'] = nrm((BATCH, SEQ, D_MODEL))
    out['x_sample'] = nrm((DEC_BATCH, DEC_SEQ, D_MODEL))
    out['cache_nsa_kv'] = nrm((DEPTH, n_pool, PAGE_SIZE, 4, NSA_KV_GROUPS, NSA_DH))
    out['cache_mla'] = nrm((DEPTH, n_pool, PAGE_SIZE, MLA_KV_RANK + MLA_ROPE))
    out['cache_nsa_win'] = nrm((DEPTH, DEC_BATCH, w_buf, 2, NSA_KV_GROUPS, NSA_DH))
    out['page_table'] = jax.random.permutation(next(ks), n_pool)[:n_used].reshape(DEC_BATCH, n_pages).astype(jnp.int32)
    out['p_prompt'] = nrm((DEPTH, BATCH, SEQ, PLE_DIM))
    out['p_sample'] = nrm((DEPTH, DEC_BATCH, DEC_SEQ, PLE_DIM))
    out['w_in'] = nrm((DEPTH, D_MODEL, N_IN), D_MODEL ** -0.5) * col_scale
    out['cmp_pe_k'] = nrm((DEPTH, CMP_BLOCK, NSA_DH), 0.1)
    out['cmp_pe_v'] = nrm((DEPTH, CMP_BLOCK, NSA_DH), 0.1)
    out['cmp_k_w1'] = nrm((DEPTH, CMP_BLOCK * NSA_DH, CMP_HIDDEN), (CMP_BLOCK * NSA_DH) ** -0.5)
    out['cmp_k_w2'] = nrm((DEPTH, CMP_HIDDEN, NSA_DH), CMP_HIDDEN ** -0.5)
    out['cmp_v_w1'] = nrm((DEPTH, CMP_BLOCK * NSA_DH, CMP_HIDDEN), (CMP_BLOCK * NSA_DH) ** -0.5)
    out['cmp_v_w2'] = nrm((DEPTH, CMP_HIDDEN, NSA_DH), CMP_HIDDEN ** -0.5)
    out['mla_q_norm'] = 1.0 + nrm((DEPTH, MLA_Q_RANK), 0.02)
    out['mla_kv_norm'] = 1.0 + nrm((DEPTH, MLA_KV_RANK), 0.02)
    out['mla_w_uq'] = nrm((DEPTH, MLA_Q_RANK, MLA_HEADS * (MLA_NOPE + MLA_ROPE)), MLA_Q_RANK ** -0.5)
    out['mla_w_uk'] = nrm((DEPTH, MLA_KV_RANK, MLA_HEADS, MLA_NOPE), MLA_KV_RANK ** -0.5)
    out['mla_w_uv'] = nrm((DEPTH, MLA_KV_RANK, MLA_HEADS, MLA_V), MLA_KV_RANK ** -0.5 * beta)
    out['w_up_a'] = nrm((DEPTH, NSA_HEADS * NSA_DH, D_MODEL), (NSA_HEADS * NSA_DH) ** -0.5 * beta)
    out['w_up_b'] = nrm((DEPTH, MLA_HEADS * MLA_V, D_MODEL), (MLA_HEADS * MLA_V) ** -0.5 * beta)
    out['w_o'] = nrm((DEPTH, D_MODEL, D_MODEL), D_MODEL ** -0.5 * beta)
    out['ln1_g'] = 1.0 + nrm((DEPTH, D_MODEL), 0.02)
    out['ln1_b'] = nrm((DEPTH, D_MODEL), 0.02)
    out['w_group'] = nrm((DEPTH, D_MODEL, N_GROUPS), D_MODEL ** -0.5)
    out['b_group'] = nrm((DEPTH, N_GROUPS), 0.01)
    out['w_router'] = nrm((DEPTH, D_MODEL, N_EXPERTS), D_MODEL ** -0.5)
    out['b_router'] = nrm((DEPTH, N_EXPERTS), 0.01)
    out['exp_w_gate'] = nrm((DEPTH, N_EXPERTS, D_MODEL, D_EXPERT), D_MODEL ** -0.5)
    out['exp_w_up'] = nrm((DEPTH, N_EXPERTS, D_MODEL, D_EXPERT), D_MODEL ** -0.5)
    out['exp_w_down'] = nrm((DEPTH, N_EXPERTS, D_EXPERT, D_MODEL), D_EXPERT ** -0.5 * beta)
    out['ln2_g'] = 1.0 + nrm((DEPTH, D_MODEL), 0.02)
    out['ln2_b'] = nrm((DEPTH, D_MODEL), 0.02)
    out['w_pe'] = nrm((DEPTH, PLE_DIM, D_MODEL), PLE_DIM ** -0.5)
    out['w_pg'] = nrm((DEPTH, D_MODEL, D_MODEL), D_MODEL ** -0.5)
    return out


def reference(x_prompt, x_sample, cache_nsa_kv, cache_mla, cache_nsa_win, page_table, p_prompt, p_sample,
              w_in, cmp_pe_k, cmp_pe_v, cmp_k_w1, cmp_k_w2, cmp_v_w1, cmp_v_w2,
              mla_q_norm, mla_kv_norm, mla_w_uq, mla_w_uk, mla_w_uv,
              w_up_a, w_up_b, w_o, ln1_g, ln1_b,
              w_group, b_group, w_router, b_router, exp_w_gate, exp_w_up, exp_w_down,
              ln2_g, ln2_b, w_pe, w_pg):
    seq = x_prompt.shape[1]
    dec = x_sample.shape[1]
    past = page_table.shape[1] * cache_nsa_kv.shape[2]
    pos_p = jnp.arange(seq, dtype=jnp.int32)
    pos_s = past + jnp.arange(dec, dtype=jnp.int32)
    hp, hs = x_prompt, x_sample
    nsa_p_l, nsa_s_l, mla_p_l, mla_s_l, win_p_l, win_s_l = [], [], [], [], [], []
    for l in range(DEPTH):
        cmp = (cmp_pe_k[l], cmp_pe_v[l], cmp_k_w1[l], cmp_k_w2[l], cmp_v_w1[l], cmp_v_w2[l])
        moe_w = (w_group[l], b_group[l], w_router[l], b_router[l], exp_w_gate[l], exp_w_up[l], exp_w_down[l])
        tail = (w_up_a[l], w_up_b[l], w_o[l], ln1_g[l], ln1_b[l], moe_w, ln2_g[l], ln2_b[l], w_pe[l], w_pg[l])
        q_p, g_p, nsa_p, win_p, ql_p, qpe_p, mla_p, ga_p, gb_p = _mixer_inputs(
            hp, pos_p, w_in[l], mla_q_norm[l], mla_kv_norm[l], mla_w_uq[l], mla_w_uk[l])
        on_p, om_p = _prompt_attention(q_p, g_p, nsa_p, win_p, ql_p, qpe_p, mla_p, mla_w_uv[l], cmp)
        hp = _layer_tail(hp, on_p, om_p, ga_p, gb_p, p_prompt[l], *tail)
        q_s, g_s, nsa_s, win_s, ql_s, qpe_s, mla_s, ga_s, gb_s = _mixer_inputs(
            hs, pos_s, w_in[l], mla_q_norm[l], mla_kv_norm[l], mla_w_uq[l], mla_w_uk[l])
        on_s, om_s = _sample_attention(q_s, g_s, nsa_s, win_s, ql_s, qpe_s, mla_s, cache_nsa_kv[l], cache_mla[l],
                                       cache_nsa_win[l], page_table, mla_w_uv[l], cmp)
        hs = _layer_tail(hs, on_s, om_s, ga_s, gb_s, p_sample[l], *tail)
        wb = cache_nsa_win.shape[2]
        nsa_p_l.append(nsa_p)
        nsa_s_l.append(nsa_s)
        mla_p_l.append(mla_p)
        mla_s_l.append(mla_s)
        win_p_l.append(win_p[:, -min(WINDOW, seq):])
        win_s_l.append(jnp.concatenate([cache_nsa_win[l], win_s], axis=1)[:, -wb:])
    return (hp, hs, jnp.stack(nsa_p_l), jnp.stack(nsa_s_l), jnp.stack(mla_p_l), jnp.stack(mla_s_l),
            jnp.stack(win_p_l), jnp.stack(win_s_l))
```

```python
import functools

import numpy as np
import jax
import jax.numpy as jnp
from jax import lax
from jax.experimental import pallas as pl
from jax.experimental.pallas import tpu as pltpu

F32 = jnp.float32
BF = jnp.bfloat16

D_MODEL = 1024
NSA_HEADS = 8
NSA_GROUPS = 2
NSA_HPG = NSA_HEADS // NSA_GROUPS
NSA_DH = 64
CMP_BLOCK = 32
CMP_STRIDE = 16
CMP_HIDDEN = 64
SEL_BLOCK = 64
SEL_TOPK = 16
WINDOW = 512
FORCE_BONUS = 1000.0
MLA_HEADS = 8
MLA_Q_RANK = 256
MLA_KV_RANK = 128
MLA_NOPE = 64
MLA_ROPE = 32
MLA_V = 64
ROPE_THETA = 10000.0
N_GROUPS = 4
EXPERTS_PER_GROUP = 8
N_EXPERTS = N_GROUPS * EXPERTS_PER_GROUP
D_EXPERT = 256
MOE_BLOCK = 128
PLE_DIM = 256
Q_BLOCK = 128
LN_EPS = 1e-5
RMS_EPS = 1e-6
KVW = NSA_GROUPS * NSA_DH
LANES = 128
HEAD_LANES = 128
MLA_LANES = 256
NSA_SCALE = NSA_DH ** -0.5
MLA_SCALE = (MLA_NOPE + MLA_ROPE) ** -0.5
NEG_INF = float("-inf")

VMEM_LIMIT = 48 * 1024 * 1024


def _cparams(sem):
    return pltpu.CompilerParams(dimension_semantics=sem, vmem_limit_bytes=VMEM_LIMIT)


def _dot(a, b):
    return jnp.dot(a, b, preferred_element_type=F32)


def _dot_nt(a, b):
    return lax.dot_general(a, b, (((1,), (1,)), ((), ())), preferred_element_type=F32)


def _rms(x, g):
    return x * lax.rsqrt(jnp.mean(x * x, axis=-1, keepdims=True) + RMS_EPS) * g


def _ln(x, g, b):
    mu = jnp.mean(x, axis=-1, keepdims=True)
    var = jnp.mean(jnp.square(x - mu), axis=-1, keepdims=True)
    return (x - mu) * lax.rsqrt(var + LN_EPS) * g + b


_C_QW = 0
_C_NSA = _C_QW + NSA_HEADS * HEAD_LANES
_C_WIN = _C_NSA + 4 * KVW
_C_GT = _C_WIN + 2 * KVW
_C_DQ = _C_GT + LANES
_C_DKV = _C_DQ + MLA_Q_RANK
_C_KR = _C_DKV + MLA_KV_RANK
_C_KRS = _C_KR + LANES
_C_END = _C_KRS + LANES
_Q2_NOPE = MLA_HEADS * MLA_NOPE
_Q2_R1 = _Q2_NOPE
_Q2_R2 = _Q2_R1 + MLA_HEADS * LANES
_Q2_END = _Q2_R2 + MLA_HEADS * LANES


def _proj_kernel(x_ref, w1_ref, tab_ref, qn_ref, kvn_ref, wq2_ref, wuk_ref,
                 qw_ref, nsa_ref, cmp_ref, selbf_ref, win_ref, winbf_ref, gates_ref, qmla_ref, mla_ref, kmla_ref):
    z = _dot(x_ref[...].astype(BF), w1_ref[...])
    qw_ref[...] = (z[:, _C_QW:_C_NSA] * NSA_SCALE).astype(BF)
    nsa = z[:, _C_NSA:_C_WIN]
    nsa_ref[...] = nsa
    cmp_ref[...] = nsa[:, :2 * KVW]
    selbf_ref[...] = nsa[:, 2 * KVW:].astype(BF)
    win = z[:, _C_WIN:_C_GT]
    win_ref[...] = win
    winbf_ref[...] = win.astype(BF)
    gates_ref[...] = z[:, _C_GT:_C_DQ]
    rn = _rms(z[:, _C_DQ:_C_DKV], qn_ref[...])
    r = _dot(rn.astype(BF), wq2_ref[...])
    lat = _dot(r[:, :_Q2_NOPE].astype(BF), wuk_ref[...])
    tab = tab_ref[...]
    c1, c2 = tab[:, 0:LANES], tab[:, LANES:2 * LANES]
    ck, sk = tab[:, 2 * LANES:3 * LANES], tab[:, 3 * LANES:4 * LANES]
    for h in range(MLA_HEADS):
        base = h * MLA_LANES
        qmla_ref[:, base:base + LANES] = lat[:, h * LANES:(h + 1) * LANES].astype(BF)
        pe = (r[:, _Q2_R1 + h * LANES:_Q2_R1 + (h + 1) * LANES] * c1
              + r[:, _Q2_R2 + h * LANES:_Q2_R2 + (h + 1) * LANES] * c2)
        qmla_ref[:, base + LANES:base + MLA_LANES] = pe.astype(BF)
    ckv = _rms(z[:, _C_DKV:_C_KR], kvn_ref[...])
    kpe = z[:, _C_KR:_C_KRS] * ck + z[:, _C_KRS:_C_END] * sk
    mla_ref[:, 0:MLA_KV_RANK] = ckv
    mla_ref[:, MLA_KV_RANK:MLA_KV_RANK + MLA_ROPE] = kpe[:, 0:MLA_ROPE]
    kmla_ref[:, 0:LANES] = ckv.astype(BF)
    kmla_ref[:, LANES:MLA_LANES] = kpe.astype(BF)


def _rope_tables(pos):
    half = MLA_ROPE // 2
    inv = ROPE_THETA ** (-jnp.arange(half, dtype=F32) / half)
    ang = pos.astype(F32)[:, None] * inv
    cos, sin = jnp.cos(ang), jnp.sin(ang)
    pad = jnp.zeros((pos.shape[0], LANES - MLA_ROPE), F32)
    return jnp.concatenate([cos, sin, pad, -sin, cos, pad, cos, cos, pad, -sin, sin, pad], axis=1)


def _arrange_proj_weights(w_in, w_uq, w_uk):
    d = w_in.shape[0]
    o = np.cumsum((0, NSA_HEADS * NSA_DH, KVW, KVW, KVW, KVW, KVW, KVW, 3 * NSA_HEADS,
                   MLA_Q_RANK, MLA_KV_RANK, MLA_ROPE, D_MODEL, D_MODEL))
    zeros = lambda n: jnp.zeros((d, n), w_in.dtype)
    cols = []
    for c in range(NSA_HEADS):
        wq = w_in[:, c * NSA_DH:(c + 1) * NSA_DH]
        cols += [wq, zeros(NSA_DH)] if c < NSA_HPG else [zeros(NSA_DH), wq]
    cols.append(w_in[:, o[1]:o[7]])
    cols += [w_in[:, o[7]:o[8]], zeros(LANES - 3 * NSA_HEADS)]
    cols.append(w_in[:, o[8]:o[10]])
    kr = w_in[:, o[10]:o[11]]
    half = MLA_ROPE // 2
    cols += [kr, zeros(LANES - MLA_ROPE), kr[:, half:], kr[:, :half], zeros(LANES - MLA_ROPE)]
    w1 = jnp.concatenate(cols, axis=1).astype(BF)
    w_ga = w_in[:, o[11]:o[12]].astype(BF)
    w_gb = w_in[:, o[12]:o[13]].astype(BF)

    uq = w_uq.reshape(MLA_Q_RANK, MLA_HEADS, MLA_NOPE + MLA_ROPE)
    zq = jnp.zeros((MLA_Q_RANK, LANES - MLA_ROPE), w_uq.dtype)
    q2 = [uq[:, :, :MLA_NOPE].reshape(MLA_Q_RANK, -1)]
    for lo in (MLA_NOPE, MLA_NOPE + half):
        for h in range(MLA_HEADS):
            x = uq[:, h, lo:lo + half]
            q2 += [x, x, zq]
    wq2 = jnp.concatenate(q2, axis=1).astype(BF)
    wuk = jnp.zeros((MLA_HEADS * MLA_NOPE, MLA_HEADS * LANES), w_uk.dtype)
    for h in range(MLA_HEADS):
        wuk = wuk.at[h * MLA_NOPE:(h + 1) * MLA_NOPE, h * LANES:(h + 1) * LANES].set(w_uk[:, h, :].T)
    return w1, w_ga, w_gb, wq2, wuk.astype(BF)


def _project(x2d, tab, tab_blocks, w1, qn, kvn, wq2, wuk, tm=256):
    n = x2d.shape[0]
    grid = (n // tm,)
    row = lambda w: pl.BlockSpec((tm, w), lambda i: (i, 0))
    full = lambda a: pl.BlockSpec(a.shape, lambda i: (0,) * a.ndim)
    outs = [(NSA_HEADS * HEAD_LANES, BF), (4 * KVW, F32), (2 * KVW, F32), (2 * KVW, BF), (2 * KVW, F32), (2 * KVW, BF),
            (LANES, F32), (MLA_HEADS * MLA_LANES, BF), (MLA_KV_RANK + MLA_ROPE, F32), (MLA_LANES, BF)]
    return pl.pallas_call(
        _proj_kernel,
        grid=grid,
        in_specs=[row(D_MODEL), full(w1), pl.BlockSpec((tm, 4 * LANES), lambda i: (i % tab_blocks, 0)),
                  full(qn), full(kvn), full(wq2), full(wuk)],
        out_specs=[row(w) for w, _ in outs],
        out_shape=[jax.ShapeDtypeStruct((n, w), dt) for w, dt in outs],
        compiler_params=_cparams(("parallel",)),
        name="proj",
    )(x2d, w1, tab, qn, kvn, wq2, wuk)


def _chunk_rows(ref, l, nchunk, tiles_per_row, tiles):
    return jnp.concatenate([ref[0, pl.ds(l * tiles_per_row + j, nchunk, stride=CMP_STRIDE * tiles_per_row), :]
                            for j in tiles], axis=1)


def _compress_partial(chunk_rows, pe_ref, w1_ref):
    a = b = None
    for l in range(CMP_STRIDE):
        xl = chunk_rows(l)
        ta = _dot((xl + pe_ref[l:l + 1, :]).astype(BF), w1_ref[l])
        tb = _dot((xl + pe_ref[CMP_STRIDE + l:CMP_STRIDE + l + 1, :]).astype(BF), w1_ref[CMP_STRIDE + l])
        a = ta if a is None else a + ta
        b = tb if b is None else b + tb
    return a, b


def _compress_kernel(x_ref, pe_ref, w1_ref, w2_ref, o_ref, b_scr):
    nchunk = o_ref.shape[1]
    a, b = _compress_partial(lambda l: _chunk_rows(x_ref, l, nchunk, 2, (0, 1)), pe_ref, w1_ref)
    b_scr[0:nchunk, :] = b
    b_scr[nchunk:nchunk + 8, :] = jnp.zeros((8, 4 * NSA_DH), F32)
    pre = a + b_scr[pl.ds(1, nchunk), :]
    o_ref[0] = _dot(jax.nn.gelu(pre).astype(BF), w2_ref[...]).astype(BF)


def _arrange_compress_weights(pe_k, pe_v, k_w1, k_w2, v_w1, v_w2):
    dh, hid = NSA_DH, CMP_HIDDEN
    pe = jnp.concatenate([pe_k, pe_k, pe_v, pe_v], axis=1)
    w1k = k_w1.reshape(CMP_BLOCK, dh, hid)
    w1v = v_w1.reshape(CMP_BLOCK, dh, hid)
    w1 = jnp.zeros((CMP_BLOCK, 4 * dh, 4 * hid), k_w1.dtype)
    w2 = jnp.zeros((4 * hid, 4 * dh), k_w2.dtype)
    for blk, (a1, a2) in enumerate(((w1k, k_w2), (w1k, k_w2), (w1v, v_w2), (w1v, v_w2))):
        w1 = w1.at[:, blk * dh:(blk + 1) * dh, blk * hid:(blk + 1) * hid].set(a1)
        w2 = w2.at[blk * hid:(blk + 1) * hid, blk * dh:(blk + 1) * dh].set(a2)
    return pe, w1.astype(BF), w2.astype(BF)


def _compress_prompt(cmp_rows, pe, w1, w2):
    b, s, _ = cmp_rows.shape
    nchunk = s // CMP_STRIDE
    full = lambda a: pl.BlockSpec(a.shape, lambda i: (0,) * a.ndim)
    cmp_rows = cmp_rows.reshape(b, 2 * s, LANES)
    return pl.pallas_call(
        _compress_kernel,
        grid=(b,),
        in_specs=[pl.BlockSpec((1, 2 * s, LANES), lambda i: (i, 0, 0)), full(pe), full(w1), full(w2)],
        out_specs=pl.BlockSpec((1, nchunk, 4 * NSA_DH), lambda i: (i, 0, 0)),
        out_shape=jax.ShapeDtypeStruct((b, nchunk, 4 * NSA_DH), BF),
        scratch_shapes=[pltpu.VMEM((nchunk + 8, 4 * NSA_DH), F32)],
        compiler_params=_cparams(("parallel",)),
        name="compress_prompt",
    )(cmp_rows, pe, w1, w2)


def _masked_softmax(s, mask):
    s = jnp.where(mask, s, NEG_INF)
    m = jnp.max(s, axis=-1, keepdims=True)
    m = jnp.where(m == NEG_INF, 0.0, m)
    e = jnp.exp(s - m)
    den = jnp.sum(e, axis=-1, keepdims=True)
    return e / jnp.where(den > 0, den, 1.0)


def _online_update(s, v, m_ref, l_ref, acc_ref):
    m_old = m_ref[...]
    m_new = jnp.maximum(m_old, jnp.max(s, axis=-1, keepdims=True))
    m_safe = jnp.where(m_new == NEG_INF, 0.0, m_new)
    alpha = jnp.exp(m_old - m_safe)
    e = jnp.exp(s - m_safe)
    l_ref[...] = alpha * l_ref[...] + jnp.sum(e, axis=-1, keepdims=True)
    acc_ref[...] = alpha * acc_ref[...] + _dot(e.astype(BF), v)
    m_ref[...] = m_new


def _online_init(m_ref, l_ref, acc_ref):
    m_ref[...] = jnp.full(m_ref.shape, NEG_INF, F32)
    l_ref[...] = jnp.zeros(l_ref.shape, F32)
    acc_ref[...] = jnp.zeros(acc_ref.shape, F32)


def _online_result(l_ref, acc_ref):
    l = l_ref[...]
    return acc_ref[...] / jnp.where(l > 0, l, 1.0)


def _alibi_slope_rows(g, rows_per_head):
    hh = lax.broadcasted_iota(jnp.int32, (NSA_HPG * rows_per_head, 1), 0) // rows_per_head
    slope = jnp.zeros(hh.shape, F32)
    for h in range(NSA_HPG):
        slope = jnp.where(hh == h, 2.0 ** -(g * NSA_HPG + h + 1), slope)
    return slope


def _select_topk(score, n_lanes):
    j = lax.broadcasted_iota(jnp.int32, score.shape, 1)
    sel = jnp.zeros(score.shape, F32)
    for _ in range(SEL_TOPK):
        m = jnp.max(score, axis=-1, keepdims=True)
        idx = jnp.min(jnp.where(score == m, j, n_lanes), axis=-1, keepdims=True)
        pick = (j == idx) & (m > NEG_INF)
        sel = jnp.where(pick, 1.0, sel)
        score = jnp.where(pick, NEG_INF, score)
    return sel


def _split_dot(x, w):
    hi = x.astype(BF)
    lo = (x - hi.astype(F32)).astype(BF)
    return _dot(hi, w) + _dot(lo, w)


SEL_TILE = 512


def _nsa_prompt_kernel(qw_ref, gt_ref, kcvc_ref, sel_ref, win_ref, ovl_ref, o_ref,
                       m_ref, l_ref, acc_ref, *, nc):
    tq = qw_ref.shape[1]
    ncp = kcvc_ref.shape[1]
    q0 = pl.program_id(1) * tq
    rows = NSA_HPG * tq
    qi = lax.broadcasted_iota(jnp.int32, (rows, 1), 0) % tq
    q_pos = q0 + qi
    qp1 = q0 + lax.broadcasted_iota(jnp.int32, (tq, 1), 0)
    kc = kcvc_ref[0, :, 0:KVW]
    vc = kcvc_ref[0, :, KVW:2 * KVW]
    gates = jax.nn.sigmoid(gt_ref[0])
    wstart = pl.multiple_of(jnp.maximum(q0 - WINDOW, 0), tq)
    wk = WINDOW + tq
    kw = win_ref[0, pl.ds(wstart, wk), 0:KVW]
    vw = win_ref[0, pl.ds(wstart, wk), KVW:2 * KVW]
    n_tiles = (q0 + tq - 1) // SEL_TILE + 1
    jblk = lax.broadcasted_iota(jnp.int32, (tq, LANES), 1)

    for g in range(NSA_GROUPS):
        q = jnp.concatenate([qw_ref[0, :, (g * NSA_HPG + h) * HEAD_LANES:(g * NSA_HPG + h + 1) * HEAD_LANES]
                             for h in range(NSA_HPG)], axis=0)
        slope = _alibi_slope_rows(g, tq)
        c = lax.broadcasted_iota(jnp.int32, (1, ncp), 1)
        dist_c = q_pos - (c * CMP_STRIDE + CMP_BLOCK - 1)
        s_c = _dot_nt(q, kc) - slope * dist_c.astype(F32)
        p_c = _masked_softmax(s_c, (dist_c >= 0) & (c < nc))
        o_c = _dot(p_c.astype(BF), vc)
        p_sum = p_c[0:tq]
        for h in range(1, NSA_HPG):
            p_sum = p_sum + p_c[h * tq:(h + 1) * tq]
        imp = _split_dot(p_sum, ovl_ref[...])
        cur = qp1 // SEL_BLOCK
        avail = jblk * SEL_BLOCK <= qp1
        forced = (jblk == 0) | (jblk == cur) | (jblk == cur - 1)
        score = jnp.where(avail, imp + FORCE_BONUS * forced.astype(F32), NEG_INF)
        sel = _select_topk(score, LANES).astype(BF)
        _online_init(m_ref, l_ref, acc_ref)

        def sel_step(t, carry):
            k0 = pl.multiple_of(t * SEL_TILE, SEL_TILE)
            kk = sel_ref[0, pl.ds(k0, SEL_TILE), 0:KVW]
            vv = sel_ref[0, pl.ds(k0, SEL_TILE), KVW:2 * KVW]
            kpos = k0 + lax.broadcasted_iota(jnp.int32, (1, SEL_TILE), 1)
            expand = (lax.broadcasted_iota(jnp.int32, (LANES, SEL_TILE), 0) == kpos // SEL_BLOCK)
            picked = _dot(sel, jnp.where(expand, 1.0, 0.0).astype(BF))
            picked = jnp.concatenate([picked] * NSA_HPG, axis=0)
            dist = q_pos - kpos
            s = _dot_nt(q, kk) - slope * dist.astype(F32)
            s = jnp.where((picked > 0.5) & (dist >= 0), s, NEG_INF)
            _online_update(s, vv, m_ref, l_ref, acc_ref)
            return carry

        lax.fori_loop(0, n_tiles, sel_step, 0)
        o_s = _online_result(l_ref, acc_ref)
        wpos = wstart + lax.broadcasted_iota(jnp.int32, (1, wk), 1)
        dist_w = q_pos - wpos
        s_w = _dot_nt(q, kw) - slope * dist_w.astype(F32)
        p_w = _masked_softmax(s_w, (dist_w >= 0) & (dist_w < WINDOW))
        o_w = _dot(p_w.astype(BF), vw)
        for h in range(NSA_HPG):
            head = g * NSA_HPG + h
            r = slice(h * tq, (h + 1) * tq)
            out = (gates[:, 3 * head:3 * head + 1] * o_c[r]
                   + gates[:, 3 * head + 1:3 * head + 2] * o_s[r]
                   + gates[:, 3 * head + 2:3 * head + 3] * o_w[r])
            o_ref[0, :, head * HEAD_LANES:(head + 1) * HEAD_LANES] = out.astype(BF)


def _overlap_matrix(ncp, nc):
    c0 = np.arange(ncp)[:, None] * CMP_STRIDE
    c1 = c0 + CMP_BLOCK - 1
    s0 = np.arange(LANES)[None, :] * SEL_BLOCK
    s1 = s0 + SEL_BLOCK - 1
    ok = (c0 <= s1) & (c1 >= s0) & (np.arange(ncp)[:, None] < nc)
    return jnp.asarray(ok.astype(np.float32)).astype(BF)


def _nsa_prompt(qw, gates, kcvc, sel_bf, win_bf):
    b, s, _ = qw.shape
    tq = Q_BLOCK
    ncp = kcvc.shape[1]
    nc = (s - CMP_BLOCK) // CMP_STRIDE + 1
    ovl = _overlap_matrix(ncp, nc)
    rows = NSA_HPG * tq
    return pl.pallas_call(
        functools.partial(_nsa_prompt_kernel, nc=nc),
        grid=(b, s // tq),
        in_specs=[pl.BlockSpec((1, tq, NSA_HEADS * HEAD_LANES), lambda bi, i: (bi, i, 0)),
                  pl.BlockSpec((1, tq, LANES), lambda bi, i: (bi, i, 0)),
                  pl.BlockSpec((1, ncp, 2 * KVW), lambda bi, i: (bi, 0, 0)),
                  pl.BlockSpec((1, s, 2 * KVW), lambda bi, i: (bi, 0, 0)),
                  pl.BlockSpec((1, s, 2 * KVW), lambda bi, i: (bi, 0, 0)),
                  pl.BlockSpec(ovl.shape, lambda bi, i: (0, 0))],
        out_specs=pl.BlockSpec((1, tq, NSA_HEADS * HEAD_LANES), lambda bi, i: (bi, i, 0)),
        out_shape=jax.ShapeDtypeStruct((b, s, NSA_HEADS * HEAD_LANES), BF),
        scratch_shapes=[pltpu.VMEM((rows, 1), F32), pltpu.VMEM((rows, 1), F32), pltpu.VMEM((rows, KVW), F32)],
        compiler_params=_cparams(("parallel", "arbitrary")),
        name="nsa_prompt",
    )(qw, gates, kcvc, sel_bf, win_bf, ovl)


MLA_TILE = 512


def _mla_prompt_kernel(q_ref, k_ref, wuv_ref, o_ref, m_ref, l_ref, acc_ref):
    tq = q_ref.shape[1]
    q0 = pl.program_id(1) * tq
    rows = MLA_HEADS * tq
    q = jnp.concatenate([q_ref[0, :, h * MLA_LANES:(h + 1) * MLA_LANES] for h in range(MLA_HEADS)], axis=0)
    q_pos = q0 + lax.broadcasted_iota(jnp.int32, (rows, 1), 0) % tq
    _online_init(m_ref, l_ref, acc_ref)

    def step(t, carry):
        k0 = pl.multiple_of(t * MLA_TILE, MLA_TILE)
        kk = k_ref[0, pl.ds(k0, MLA_TILE), :]
        kpos = k0 + lax.broadcasted_iota(jnp.int32, (1, MLA_TILE), 1)
        s = jnp.where(kpos <= q_pos, _dot_nt(q, kk) * MLA_SCALE, NEG_INF)
        _online_update(s, kk, m_ref, l_ref, acc_ref)
        return carry

    lax.fori_loop(0, (q0 + tq - 1) // MLA_TILE + 1, step, 0)
    o_lat = _online_result(l_ref, acc_ref)[:, 0:MLA_KV_RANK].astype(BF)
    out = _dot(o_lat[0:tq], wuv_ref[0])
    for h in range(1, MLA_HEADS):
        out = out + _dot(o_lat[h * tq:(h + 1) * tq], wuv_ref[h])
    o_ref[0] = out.astype(BF)


def _arrange_wuv(w_uv):
    w = jnp.zeros((MLA_HEADS, MLA_KV_RANK, MLA_HEADS * MLA_V), w_uv.dtype)
    for h in range(MLA_HEADS):
        w = w.at[h, :, h * MLA_V:(h + 1) * MLA_V].set(w_uv[:, h, :])
    return w.astype(BF)


def _mla_prompt(qmla, kmla, wuv):
    b, s, _ = qmla.shape
    tq = Q_BLOCK
    rows = MLA_HEADS * tq
    return pl.pallas_call(
        _mla_prompt_kernel,
        grid=(b, s // tq),
        in_specs=[pl.BlockSpec((1, tq, MLA_HEADS * MLA_LANES), lambda bi, i: (bi, i, 0)),
                  pl.BlockSpec((1, s, MLA_LANES), lambda bi, i: (bi, 0, 0)),
                  pl.BlockSpec(wuv.shape, lambda bi, i: (0, 0, 0))],
        out_specs=pl.BlockSpec((1, tq, MLA_HEADS * MLA_V), lambda bi, i: (bi, i, 0)),
        out_shape=jax.ShapeDtypeStruct((b, s, MLA_HEADS * MLA_V), BF),
        scratch_shapes=[pltpu.VMEM((rows, 1), F32), pltpu.VMEM((rows, 1), F32), pltpu.VMEM((rows, MLA_LANES), F32)],
        compiler_params=_cparams(("parallel", "arbitrary")),
        name="mla_prompt",
    )(qmla, kmla, wuv)


_R_EXPERT0 = 32


def _merge_kernel(x_ref, on_ref, om_ref, wga_ref, wgb_ref, wa_ref, wb_ref, wo_ref, g1_ref, b1_ref,
                  wr_ref, br_ref, h_ref, route_ref, *, alpha):
    x = x_ref[...]
    xb = x.astype(BF)
    ga = jax.nn.sigmoid(_dot(xb, wga_ref[...]))
    gb = jax.nn.sigmoid(_dot(xb, wgb_ref[...]))
    mix = ga * _dot(on_ref[...], wa_ref[...]) + gb * _dot(om_ref[...], wb_ref[...])
    h = _ln(alpha * x + _dot(mix.astype(BF), wo_ref[...]), g1_ref[...], b1_ref[...])
    h_ref[...] = h
    logit = _dot(h.astype(BF), wr_ref[...]) + br_ref[...]
    lane = lax.broadcasted_iota(jnp.int32, logit.shape, 1)
    is_g = lane < N_GROUPS
    gl = jnp.where(is_g, logit, NEG_INF)
    gmax = jnp.max(gl, axis=-1, keepdims=True)
    gsel = jnp.min(jnp.where(gl == gmax, lane, LANES), axis=-1, keepdims=True)
    g_w = 1.0 / jnp.sum(jnp.exp(gl - gmax), axis=-1, keepdims=True)
    e_lane = lane - _R_EXPERT0
    in_grp = (e_lane >= 0) & (e_lane < N_EXPERTS) & (e_lane // EXPERTS_PER_GROUP == gsel)
    el = jnp.where(in_grp, logit, NEG_INF)
    v1 = jnp.max(el, axis=-1, keepdims=True)
    i1 = jnp.min(jnp.where(el == v1, e_lane, LANES), axis=-1, keepdims=True)
    el2 = jnp.where(e_lane == i1, NEG_INF, el)
    v2 = jnp.max(el2, axis=-1, keepdims=True)
    i2 = jnp.min(jnp.where(el2 == v2, e_lane, LANES), axis=-1, keepdims=True)
    e2 = jnp.exp(v2 - v1)
    den = 1.0 + e2
    route = jnp.where(lane == 0, i1.astype(F32),
                      jnp.where(lane == 1, i2.astype(F32),
                                jnp.where(lane == 2, g_w * (1.0 / den),
                                          jnp.where(lane == 3, g_w * (e2 / den), 0.0))))
    route_ref[...] = route


def _arrange_merge_weights(w_up_a, w_group, b_group, w_router, b_router):
    wa = jnp.zeros((NSA_HEADS * HEAD_LANES, D_MODEL), w_up_a.dtype)
    for c in range(NSA_HEADS):
        off = c * HEAD_LANES + (0 if c < NSA_HPG else NSA_DH)
        wa = wa.at[off:off + NSA_DH].set(w_up_a[c * NSA_DH:(c + 1) * NSA_DH])
    d = w_group.shape[0]
    wr = jnp.concatenate([w_group, jnp.zeros((d, _R_EXPERT0 - N_GROUPS), w_group.dtype), w_router,
                          jnp.zeros((d, LANES - _R_EXPERT0 - N_EXPERTS), w_group.dtype)], axis=1)
    br = jnp.concatenate([b_group, jnp.zeros((_R_EXPERT0 - N_GROUPS,), b_group.dtype), b_router,
                          jnp.zeros((LANES - _R_EXPERT0 - N_EXPERTS,), b_group.dtype)])[None]
    return wa.astype(BF), wr.astype(BF), br


def _merge(x2d, on, om, w_ga, w_gb, wa, wb, wo, g1, b1, wr, br, alpha, tm=256):
    n = x2d.shape[0]
    row = lambda w: pl.BlockSpec((tm, w), lambda i: (i, 0))
    full = lambda a: pl.BlockSpec(a.shape, lambda i: (0,) * a.ndim)
    consts = (w_ga, w_gb, wa, wb, wo, g1, b1, wr, br)
    return pl.pallas_call(
        functools.partial(_merge_kernel, alpha=alpha),
        grid=(n // tm,),
        in_specs=[row(D_MODEL), row(on.shape[1]), row(om.shape[1])] + [full(a) for a in consts],
        out_specs=[row(D_MODEL), row(LANES)],
        out_shape=[jax.ShapeDtypeStruct((n, D_MODEL), F32), jax.ShapeDtypeStruct((n, LANES), F32)],
        compiler_params=_cparams(("parallel",)),
        name="merge_ln1_route",
    )(x2d, on, om, *consts)


def _expert_kernel(blk_e_ref, src_ref, dst_ref, h_hbm, wg_ref, wu_ref, wd_ref, y_hbm, xbuf, ybuf, gsem, ssem,
                   *, n_rows):
    base = pl.program_id(0) * MOE_BLOCK

    def gather(r):
        return pltpu.make_async_copy(h_hbm.at[pl.ds(src_ref[base + r], 1)], xbuf.at[pl.ds(r, 1)], gsem)

    def scatter(r):
        return pltpu.make_async_copy(ybuf.at[pl.ds(r, 1)], y_hbm.at[pl.ds(dst_ref[base + r], 1)], ssem)

    def start_gather(r, c):
        gather(r).start()
        return c

    def wait_gather(r, c):
        gather(r).wait()
        return c

    lax.fori_loop(0, MOE_BLOCK, start_gather, 0)
    lax.fori_loop(0, MOE_BLOCK, wait_gather, 0)
    xb = xbuf[...].astype(BF)
    hid = jax.nn.silu(_dot(xb, wg_ref[0])) * _dot(xb, wu_ref[0])
    ybuf[...] = _dot(hid.astype(BF), wd_ref[0])

    def start_scatter(r, c):
        @pl.when(dst_ref[base + r] < n_rows)
        def _():
            scatter(r).start()
        return c

    def wait_scatter(r, c):
        @pl.when(dst_ref[base + r] < n_rows)
        def _():
            scatter(r).wait()
        return c

    lax.fori_loop(0, MOE_BLOCK, start_scatter, 0)
    lax.fori_loop(0, MOE_BLOCK, wait_scatter, 0)


def _experts(h, blk_e, src, dst, wg, wu, wd):
    n, d = h.shape
    nb = blk_e.shape[0]
    wspec = lambda a: pl.BlockSpec((1,) + a.shape[1:], lambda i, e, s, t: (e[i], 0, 0))
    return pl.pallas_call(
        functools.partial(_expert_kernel, n_rows=2 * n),
        grid_spec=pltpu.PrefetchScalarGridSpec(
            num_scalar_prefetch=3,
            grid=(nb,),
            in_specs=[pl.BlockSpec(memory_space=pl.ANY), wspec(wg), wspec(wu), wspec(wd)],
            out_specs=pl.BlockSpec(memory_space=pl.ANY),
            scratch_shapes=[pltpu.VMEM((MOE_BLOCK, d), F32), pltpu.VMEM((MOE_BLOCK, d), F32),
                            pltpu.SemaphoreType.DMA, pltpu.SemaphoreType.DMA]),
        out_shape=jax.ShapeDtypeStruct((2 * n, d), F32),
        compiler_params=_cparams(("arbitrary",)),
        name="experts",
    )(blk_e, src, dst, h, wg, wu, wd)


def _route_plan(route, n):
    eid = route[:, 0:2].astype(jnp.int32).reshape(-1)
    m = eid.shape[0]
    order = jnp.argsort(eid)
    e_sorted = eid[order]
    counts = jnp.bincount(eid, length=N_EXPERTS)
    padded = (counts + MOE_BLOCK - 1) // MOE_BLOCK * MOE_BLOCK
    pad_end = jnp.cumsum(padded)
    start = jnp.cumsum(counts) - counts
    slot = (pad_end - padded)[e_sorted] + jnp.arange(m) - start[e_sorted]
    nb = -(-m // MOE_BLOCK) + N_EXPERTS
    src = jnp.zeros((nb * MOE_BLOCK,), jnp.int32).at[slot].set((order // 2).astype(jnp.int32))
    dst = jnp.full((nb * MOE_BLOCK,), m, jnp.int32).at[slot].set(order.astype(jnp.int32))
    blk_e = jnp.minimum(jnp.sum(pad_end[None, :] <= (jnp.arange(nb) * MOE_BLOCK)[:, None], axis=1),
                        N_EXPERTS - 1).astype(jnp.int32)
    return blk_e, src, dst


def _final_kernel(h_ref, y_ref, route_ref, g2_ref, b2_ref, wpg_ref, p_ref, wpe_ref, o_ref, *, alpha):
    h = h_ref[...]
    route = route_ref[...]
    moe = route[:, 2:3] * y_ref[:, 0:D_MODEL] + route[:, 3:4] * y_ref[:, D_MODEL:2 * D_MODEL]
    h2 = _ln(alpha * h + moe, g2_ref[...], b2_ref[...])
    gate = jax.nn.sigmoid(_dot(h2.astype(BF), wpg_ref[...]))
    o_ref[...] = h2 + gate * _dot(p_ref[...].astype(BF), wpe_ref[...])


def _final(h, y2, route, g2, b2, wpg, p2d, wpe, alpha, tm=256):
    n = h.shape[0]
    row = lambda w: pl.BlockSpec((tm, w), lambda i: (i, 0))
    full = lambda a: pl.BlockSpec(a.shape, lambda i: (0,) * a.ndim)
    return pl.pallas_call(
        functools.partial(_final_kernel, alpha=alpha),
        grid=(n // tm,),
        in_specs=[row(D_MODEL), row(2 * D_MODEL), row(LANES), full(g2), full(b2), full(wpg), row(PLE_DIM), full(wpe)],
        out_specs=row(D_MODEL),
        out_shape=jax.ShapeDtypeStruct((n, D_MODEL), F32),
        compiler_params=_cparams(("parallel",)),
        name="combine_ln2_ple",
    )(h, y2, route, g2, b2, wpg, p2d, wpe)


def _layer_tail(x2d, on, om, p2d, tw, alpha, tm=256):
    n = x2d.shape[0]
    h, route = _merge(x2d, on, om, tw["w_ga"], tw["w_gb"], tw["wa"], tw["wb"], tw["wo"], tw["g1"], tw["b1"],
                      tw["wr"], tw["br"], alpha, tm=tm)
    blk_e, src, dst = _route_plan(route, n)
    y = _experts(h, blk_e, src, dst, tw["wg"], tw["wu"], tw["wd"])
    return _final(h, y.reshape(n, 2 * D_MODEL), route, tw["g2"], tw["b2"], tw["wpg"], p2d, tw["wpe"], alpha, tm=tm)


SAMPLE_PAGES_PER_STEP = 16
CMP_PAGES_PER_STEP = 64


def _stack_heads(ref, n_heads, lanes):
    return jnp.concatenate([ref[0, :, c * lanes:(c + 1) * lanes] for c in range(n_heads)], axis=0)


def _alibi_rows(first_head, n_heads, rows_per_head):
    hh = lax.broadcasted_iota(jnp.int32, (n_heads * rows_per_head, 1), 0) // rows_per_head
    slope = jnp.zeros(hh.shape, F32)
    for h in range(n_heads):
        slope = jnp.where(hh == h, 2.0 ** -(first_head + h + 1), slope)
    return slope


def _pad_rows(x, n):
    return jnp.concatenate([x, jnp.zeros((n - x.shape[0], x.shape[1]), x.dtype)], axis=0)


def _nsa_sample_cmp_kernel(pt_ref, qw_ref, pe_ref, w1_ref, w2_ref, ovl_ref, *rest, pps, past, nc):
    pages = rest[:pps]
    oc_ref, sel_ref, a_scr, b_scr = rest[pps:]
    step = pl.program_id(1)
    cpp = pages[0].shape[1] // (4 * CMP_STRIDE)
    nstep = pps * cpp
    a, b = _compress_partial(
        lambda l: jnp.concatenate([_chunk_rows(pg, l, cpp, 4, (0, 1)) for pg in pages], axis=0), pe_ref, w1_ref)
    r0 = pl.multiple_of(step * nstep, nstep)
    a_scr[pl.ds(r0, nstep), :] = a
    b_scr[pl.ds(r0, nstep), :] = b

    @pl.when(step == pl.num_programs(1) - 1)
    def _():
        _nsa_sample_select(qw_ref, w2_ref, ovl_ref, oc_ref, sel_ref, a_scr, b_scr, past=past, nc=nc)


def _nsa_sample_select(qw_ref, w2_ref, ovl_ref, oc_ref, sel_ref, a_scr, b_scr, *, past, nc):
    dec = qw_ref.shape[1]
    nchunk = a_scr.shape[0]
    nsl = sel_ref.shape[2]
    b_scr[nchunk:nchunk + 8, :] = jnp.zeros((8, 4 * NSA_DH), F32)
    pre = a_scr[...] + b_scr[pl.ds(1, nchunk), :]
    kcvc = _dot(jax.nn.gelu(pre).astype(BF), w2_ref[...]).astype(BF)
    kc = kcvc[:, 0:KVW]
    vc = kcvc[:, KVW:2 * KVW]
    rows = NSA_HPG * dec
    q_pos = past + lax.broadcasted_iota(jnp.int32, (rows, 1), 0) % dec
    qp1 = past + lax.broadcasted_iota(jnp.int32, (dec, 1), 0)
    c = lax.broadcasted_iota(jnp.int32, (1, nchunk), 1)
    jblk = lax.broadcasted_iota(jnp.int32, (dec, nsl), 1)
    for g in range(NSA_GROUPS):
        q = jnp.concatenate([qw_ref[0, :, (g * NSA_HPG + h) * HEAD_LANES:(g * NSA_HPG + h + 1) * HEAD_LANES]
                             for h in range(NSA_HPG)], axis=0)
        slope = _alibi_rows(g * NSA_HPG, NSA_HPG, dec)
        dist_c = q_pos - (c * CMP_STRIDE + CMP_BLOCK - 1)
        s_c = _dot_nt(q, kc) - slope * dist_c.astype(F32)
        p_c = _masked_softmax(s_c, (dist_c >= 0) & (c < nc))
        oc_ref[0, g * rows:(g + 1) * rows, :] = _dot(p_c.astype(BF), vc)
        p_sum = p_c[0:dec]
        for h in range(1, NSA_HPG):
            p_sum = p_sum + p_c[h * dec:(h + 1) * dec]
        imp = _split_dot(p_sum, ovl_ref[...])
        cur = qp1 // SEL_BLOCK
        avail = jblk * SEL_BLOCK <= qp1
        forced = (jblk == 0) | (jblk == cur) | (jblk == cur - 1)
        score = jnp.where(avail, imp + FORCE_BONUS * forced.astype(F32), NEG_INF)
        sel = _select_topk(score, nsl)
        sel_ref[0, g * rows:(g + 1) * rows, :] = jnp.concatenate([sel] * NSA_HPG, axis=0)


def _overlap_matrix_wide(ncp, nc, nsl):
    c0 = np.arange(ncp)[:, None] * CMP_STRIDE
    c1 = c0 + CMP_BLOCK - 1
    s0 = np.arange(nsl)[None, :] * SEL_BLOCK
    s1 = s0 + SEL_BLOCK - 1
    ok = (c0 <= s1) & (c1 >= s0) & (np.arange(ncp)[:, None] < nc)
    return jnp.asarray(ok.astype(np.float32)).astype(BF)


def _nsa_sample_cmp(page_table, pool, qw, pe, w1, w2, past):
    db, n_pages = page_table.shape
    page = pool.shape[1]
    dec = qw.shape[1]
    nchunk = n_pages * page // CMP_STRIDE
    nc = (past + dec - CMP_BLOCK) // CMP_STRIDE + 1
    ns = -(-(past + dec) // SEL_BLOCK)
    nsl = -(-ns // LANES) * LANES
    ovl = _overlap_matrix_wide(nchunk, nc, nsl)
    rows = NSA_HEADS * dec
    pps = min(CMP_PAGES_PER_STEP, n_pages)
    tiles = pool.shape[2] // LANES
    pool = pool.reshape(pool.shape[0], page * tiles, LANES)
    full = lambda a: pl.BlockSpec(a.shape, lambda s, j, pt: (0,) * a.ndim)
    page_spec = lambda k: pl.BlockSpec((1, page * tiles, LANES), lambda s, j, pt: (pt[s, j * pps + k], 0, 0))
    return pl.pallas_call(
        functools.partial(_nsa_sample_cmp_kernel, pps=pps, past=past, nc=nc),
        grid_spec=pltpu.PrefetchScalarGridSpec(
            num_scalar_prefetch=1,
            grid=(db, n_pages // pps),
            in_specs=[pl.BlockSpec((1, dec, NSA_HEADS * HEAD_LANES), lambda s, j, pt: (s, 0, 0)),
                      full(pe), full(w1), full(w2), full(ovl)] + [page_spec(k) for k in range(pps)],
            out_specs=[pl.BlockSpec((1, rows, KVW), lambda s, j, pt: (s, 0, 0)),
                       pl.BlockSpec((1, rows, nsl), lambda s, j, pt: (s, 0, 0))],
            scratch_shapes=[pltpu.VMEM((nchunk, 4 * NSA_DH), F32), pltpu.VMEM((nchunk + 8, 4 * NSA_DH), F32)]),
        out_shape=[jax.ShapeDtypeStruct((db, rows, KVW), F32), jax.ShapeDtypeStruct((db, rows, nsl), F32)],
        compiler_params=pltpu.CompilerParams(dimension_semantics=("parallel", "arbitrary"),
                                             vmem_limit_bytes=56 * 1024 * 1024),
        name="nsa_sample_compress_select",
    )(page_table, qw, pe, w1, w2, ovl, *([pool] * pps))


def _nsa_sample_sel_kernel(pt_ref, qw_ref, gt_ref, oc_ref, selt_ref, self_ref, new_ref, winold_ref, winnew_ref,
                           *rest, pps, past):
    pages = rest[:pps]
    o_ref, m_ref, l_ref, acc_ref = rest[pps:]
    step = pl.program_id(1)
    dec = qw_ref.shape[1]
    page = pages[0].shape[1]
    rows = NSA_HEADS * dec
    nk = pps * page

    @pl.when(step == 0)
    def _():
        _online_init(m_ref, l_ref, acc_ref)

    q = _stack_heads(qw_ref, NSA_HEADS, HEAD_LANES)
    slope = _alibi_rows(0, NSA_HEADS, dec)
    q_pos = past + lax.broadcasted_iota(jnp.int32, (rows, 1), 0) % dec
    kk = jnp.concatenate([pg[0, :, 0:KVW] for pg in pages], axis=0).astype(BF)
    vv = jnp.concatenate([pg[0, :, KVW:2 * KVW] for pg in pages], axis=0).astype(BF)
    key = lax.broadcasted_iota(jnp.int32, (1, nk), 1)
    kpos = step * nk + key
    blocks_per_step = nk // SEL_BLOCK
    tile_off = (step % (LANES // blocks_per_step)) * blocks_per_step
    expand = lax.broadcasted_iota(jnp.int32, (LANES, nk), 0) == tile_off + key // SEL_BLOCK
    picked = _dot(selt_ref[0].astype(BF), jnp.where(expand, 1.0, 0.0).astype(BF))
    dist = q_pos - kpos
    s = _dot_nt(q, kk) - slope * dist.astype(F32)
    _online_update(jnp.where((picked > 0.5) & (dist >= 0), s, NEG_INF), vv, m_ref, l_ref, acc_ref)

    @pl.when(step == pl.num_programs(1) - 1)
    def _():
        inew = lax.broadcasted_iota(jnp.int32, (1, LANES), 1)
        dist_n = q_pos - (past + inew)
        valid_n = (inew < dec) & (dist_n >= 0)
        new = _pad_rows(new_ref[0], LANES)
        blk_new = past // SEL_BLOCK
        picked_n = self_ref[0][:, blk_new:blk_new + 1]
        s_n = _dot_nt(q, new[:, 0:KVW].astype(BF)) - slope * dist_n.astype(F32)
        _online_update(jnp.where(valid_n & (picked_n > 0.5), s_n, NEG_INF), new[:, KVW:2 * KVW].astype(BF),
                       m_ref, l_ref, acc_ref)
        o_s = _online_result(l_ref, acc_ref)
        wb = winold_ref.shape[1]
        wpos = past - wb + lax.broadcasted_iota(jnp.int32, (1, wb), 1)
        dist_o = q_pos - wpos
        s_o = _dot_nt(q, winold_ref[0, :, 0:KVW].astype(BF)) - slope * dist_o.astype(F32)
        s_o = jnp.where((dist_o >= 0) & (dist_o < WINDOW) & (wpos >= 0), s_o, NEG_INF)
        wnew = _pad_rows(winnew_ref[0], LANES)
        s_w = _dot_nt(q, wnew[:, 0:KVW].astype(BF)) - slope * dist_n.astype(F32)
        s_w = jnp.where(valid_n & (dist_n < WINDOW), s_w, NEG_INF)
        mw = jnp.maximum(jnp.max(s_o, axis=-1, keepdims=True), jnp.max(s_w, axis=-1, keepdims=True))
        mw = jnp.where(mw == NEG_INF, 0.0, mw)
        e_o = jnp.exp(s_o - mw)
        e_w = jnp.exp(s_w - mw)
        den = jnp.sum(e_o, axis=-1, keepdims=True) + jnp.sum(e_w, axis=-1, keepdims=True)
        o_w = (_dot(e_o.astype(BF), winold_ref[0, :, KVW:2 * KVW].astype(BF))
               + _dot(e_w.astype(BF), wnew[:, KVW:2 * KVW].astype(BF))) / jnp.where(den > 0, den, 1.0)
        gates = jax.nn.sigmoid(gt_ref[0])
        o_c = oc_ref[0]
        for head in range(NSA_HEADS):
            r = slice(head * dec, (head + 1) * dec)
            out = (gates[:, 3 * head:3 * head + 1] * o_c[r] + gates[:, 3 * head + 1:3 * head + 2] * o_s[r]
                   + gates[:, 3 * head + 2:3 * head + 3] * o_w[r])
            o_ref[0, :, head * HEAD_LANES:(head + 1) * HEAD_LANES] = out.astype(BF)


def _nsa_sample_sel(page_table, pool, qw, gates, o_c, sel, new_rows, win_old, win_new, past):
    db, n_pages = page_table.shape
    page = pool.shape[1]
    dec = qw.shape[1]
    pps = min(SAMPLE_PAGES_PER_STEP, n_pages)
    rows = NSA_HEADS * dec
    nsl = sel.shape[2]
    steps_per_tile = LANES // (pps * page // SEL_BLOCK)
    blk3 = lambda a: pl.BlockSpec((1,) + a.shape[1:], lambda s, j, pt: (s, 0, 0))
    page_spec = lambda k: pl.BlockSpec((1, page, 2 * KVW), lambda s, j, pt: (pt[s, j * pps + k], 0, 1))
    return pl.pallas_call(
        functools.partial(_nsa_sample_sel_kernel, pps=pps, past=past),
        grid_spec=pltpu.PrefetchScalarGridSpec(
            num_scalar_prefetch=1,
            grid=(db, n_pages // pps),
            in_specs=[blk3(qw), blk3(gates), blk3(o_c),
                      pl.BlockSpec((1, rows, LANES), lambda s, j, pt: (s, 0, j // steps_per_tile)),
                      blk3(sel), blk3(new_rows), blk3(win_old), blk3(win_new)] + [page_spec(k) for k in range(pps)],
            out_specs=pl.BlockSpec((1, dec, NSA_HEADS * HEAD_LANES), lambda s, j, pt: (s, 0, 0)),
            scratch_shapes=[pltpu.VMEM((rows, 1), F32), pltpu.VMEM((rows, 1), F32), pltpu.VMEM((rows, KVW), F32)]),
        out_shape=jax.ShapeDtypeStruct((db, dec, NSA_HEADS * HEAD_LANES), BF),
        compiler_params=_cparams(("parallel", "arbitrary")),
        name="nsa_sample_select_window",
    )(page_table, qw, gates, o_c, sel, sel, new_rows, win_old, win_new, *([pool] * pps))


def _mla_sample_kernel(pt_ref, q_ref, new_ref, wuv_ref, *rest, pps):
    pages = rest[:pps]
    o_ref, m_ref, l_ref, acc_ref = rest[pps:]
    step = pl.program_id(1)
    dec = q_ref.shape[1]
    kvw = MLA_KV_RANK + MLA_ROPE

    @pl.when(step == 0)
    def _():
        _online_init(m_ref, l_ref, acc_ref)

    q = _stack_heads(q_ref, MLA_HEADS, MLA_LANES)[:, 0:kvw]
    kk = jnp.concatenate([pg[0] for pg in pages], axis=0).astype(BF)
    _online_update(_dot_nt(q, kk) * MLA_SCALE, kk, m_ref, l_ref, acc_ref)

    @pl.when(step == pl.num_programs(1) - 1)
    def _():
        rows = MLA_HEADS * dec
        qi = lax.broadcasted_iota(jnp.int32, (rows, 1), 0) % dec
        inew = lax.broadcasted_iota(jnp.int32, (1, LANES), 1)
        new = _pad_rows(new_ref[0], LANES).astype(BF)
        s_n = jnp.where((inew < dec) & (inew <= qi), _dot_nt(q, new) * MLA_SCALE, NEG_INF)
        _online_update(s_n, new, m_ref, l_ref, acc_ref)
        o_lat = _online_result(l_ref, acc_ref)[:, 0:MLA_KV_RANK].astype(BF)
        out = _dot(o_lat[0:dec], wuv_ref[0])
        for h in range(1, MLA_HEADS):
            out = out + _dot(o_lat[h * dec:(h + 1) * dec], wuv_ref[h])
        o_ref[0] = out.astype(BF)


def _mla_sample(page_table, pool, qmla, new_rows, wuv):
    db, n_pages = page_table.shape
    page, kvw = pool.shape[1:]
    dec = qmla.shape[1]
    pps = min(SAMPLE_PAGES_PER_STEP, n_pages)
    rows = MLA_HEADS * dec
    blk3 = lambda a: pl.BlockSpec((1,) + a.shape[1:], lambda s, j, pt: (s, 0, 0))
    page_spec = lambda k: pl.BlockSpec((1, page, kvw), lambda s, j, pt: (pt[s, j * pps + k], 0, 0))
    return pl.pallas_call(
        functools.partial(_mla_sample_kernel, pps=pps),
        grid_spec=pltpu.PrefetchScalarGridSpec(
            num_scalar_prefetch=1,
            grid=(db, n_pages // pps),
            in_specs=[blk3(qmla), blk3(new_rows), pl.BlockSpec(wuv.shape, lambda s, j, pt: (0, 0, 0))]
                     + [page_spec(k) for k in range(pps)],
            out_specs=pl.BlockSpec((1, dec, MLA_HEADS * MLA_V), lambda s, j, pt: (s, 0, 0)),
            scratch_shapes=[pltpu.VMEM((rows, 1), F32), pltpu.VMEM((rows, 1), F32), pltpu.VMEM((rows, kvw), F32)]),
        out_shape=jax.ShapeDtypeStruct((db, dec, MLA_HEADS * MLA_V), BF),
        compiler_params=_cparams(("parallel", "arbitrary")),
        name="mla_sample",
    )(page_table, qmla, new_rows, wuv, *([pool] * pps))


def _tail_weights(w_in_parts, w_up_a, w_up_b, w_o, ln1_g, ln1_b, w_group, b_group, w_router, b_router,
                  exp_w_gate, exp_w_up, exp_w_down, ln2_g, ln2_b, w_pe, w_pg):
    w_ga, w_gb = w_in_parts
    wa, wr, br = _arrange_merge_weights(w_up_a, w_group, b_group, w_router, b_router)
    return dict(w_ga=w_ga, w_gb=w_gb, wa=wa, wb=w_up_b.astype(BF), wo=w_o.astype(BF), g1=ln1_g[None], b1=ln1_b[None],
                wr=wr, br=br, wg=exp_w_gate.astype(BF), wu=exp_w_up.astype(BF), wd=exp_w_down.astype(BF),
                g2=ln2_g[None], b2=ln2_b[None], wpg=w_pg.astype(BF), wpe=w_pe.astype(BF))


def kernel(x_prompt, x_sample, cache_nsa_kv, cache_mla, cache_nsa_win, page_table, p_prompt, p_sample,
           w_in, cmp_pe_k, cmp_pe_v, cmp_k_w1, cmp_k_w2, cmp_v_w1, cmp_v_w2,
           mla_q_norm, mla_kv_norm, mla_w_uq, mla_w_uk, mla_w_uv,
           w_up_a, w_up_b, w_o, ln1_g, ln1_b,
           w_group, b_group, w_router, b_router, exp_w_gate, exp_w_up, exp_w_down,
           ln2_g, ln2_b, w_pe, w_pg):
    depth = w_in.shape[0]
    b, s, d = x_prompt.shape
    db, dec, _ = x_sample.shape
    n_pool, page = cache_nsa_kv.shape[1:3]
    past = page_table.shape[1] * page
    wb = cache_nsa_win.shape[2]
    alpha = (2.0 * depth) ** 0.25
    np_, ns_ = b * s, db * dec
    tm_p, tm_s = min(256, np_), min(256, ns_)
    tab_p = _rope_tables(jnp.arange(s, dtype=jnp.int32))
    tab_s = jnp.tile(_rope_tables(past + jnp.arange(dec, dtype=jnp.int32)), (tm_s // dec, 1))
    page_table = page_table.astype(jnp.int32)
    hp = x_prompt.reshape(np_, d)
    hs = x_sample.reshape(ns_, d)
    nsa_p_l, nsa_s_l, mla_p_l, mla_s_l, win_p_l, win_s_l = [], [], [], [], [], []
    for l in range(depth):
        w1, w_ga, w_gb, wq2, wuk = _arrange_proj_weights(w_in[l], mla_w_uq[l], mla_w_uk[l])
        pe, cw1, cw2 = _arrange_compress_weights(cmp_pe_k[l], cmp_pe_v[l], cmp_k_w1[l], cmp_k_w2[l],
                                                 cmp_v_w1[l], cmp_v_w2[l])
        wuv = _arrange_wuv(mla_w_uv[l])
        tw = _tail_weights((w_ga, w_gb), w_up_a[l], w_up_b[l], w_o[l], ln1_g[l], ln1_b[l], w_group[l], b_group[l],
                           w_router[l], b_router[l], exp_w_gate[l], exp_w_up[l], exp_w_down[l],
                           ln2_g[l], ln2_b[l], w_pe[l], w_pg[l])
        qn, kvn = mla_q_norm[l][None], mla_kv_norm[l][None]
        qw, nsa, cmp, selbf, win, winbf, gt, qmla, mla, kmla = _project(hp, tab_p, s // tm_p, w1, qn, kvn, wq2, wuk,
                                                                         tm=tm_p)
        r3 = lambda a: a.reshape(b, s, -1)
        kcvc = _compress_prompt(r3(cmp), pe, cw1, cw2)
        on = _nsa_prompt(r3(qw), r3(gt), kcvc, r3(selbf), r3(winbf)).reshape(np_, -1)
        om = _mla_prompt(r3(qmla), r3(kmla), wuv).reshape(np_, -1)
        hp = _layer_tail(hp, on, om, p_prompt[l].reshape(np_, -1), tw, alpha, tm=tm_p)
        nsa_p_l.append(nsa.reshape(b, s, 4, NSA_GROUPS, NSA_DH))
        mla_p_l.append(r3(mla))
        win_p_l.append(win.reshape(b, s, 2, NSA_GROUPS, NSA_DH)[:, -min(WINDOW, s):])
        qw, nsa, cmp, selbf, win, winbf, gt, qmla, mla, kmla = _project(hs, tab_s, 1, w1, qn, kvn, wq2, wuk, tm=tm_s)
        r3 = lambda a: a.reshape(db, dec, -1)
        pool_nsa = cache_nsa_kv[l].reshape(n_pool, page, 4 * KVW)
        win_old = cache_nsa_win[l].reshape(db, wb, 2 * KVW)
        o_c, sel = _nsa_sample_cmp(page_table, pool_nsa, r3(qw), pe, cw1, cw2, past)
        on = _nsa_sample_sel(page_table, pool_nsa, r3(qw), r3(gt), o_c, sel, r3(nsa)[:, :, 2 * KVW:], win_old, r3(win),
                             past).reshape(ns_, -1)
        om = _mla_sample(page_table, cache_mla[l], r3(qmla), r3(mla), wuv).reshape(ns_, -1)
        hs = _layer_tail(hs, on, om, p_sample[l].reshape(ns_, -1), tw, alpha, tm=tm_s)
        nsa_s_l.append(nsa.reshape(db, dec, 4, NSA_GROUPS, NSA_DH))
        mla_s_l.append(r3(mla))
        win_new = win.reshape(db, dec, 2, NSA_GROUPS, NSA_DH)
        win_s_l.append(jnp.concatenate([cache_nsa_win[l], win_new], axis=1)[:, -wb:])
    return (hp.reshape(b, s, d), hs.reshape(db, dec, d), jnp.stack(nsa_p_l), jnp.stack(nsa_s_l),
            jnp.stack(mla_p_l), jnp.stack(mla_s_l), jnp.stack(win_p_l), jnp.stack(win_s_l))
```

```python
import functools

import numpy as np
import jax
import jax.numpy as jnp
from jax import lax
from jax.experimental import pallas as pl
from jax.experimental.pallas import tpu as pltpu

F32 = jnp.float32
BF = jnp.bfloat16

D_MODEL = 1024
NSA_HEADS = 8
NSA_GROUPS = 2
NSA_HPG = NSA_HEADS // NSA_GROUPS
NSA_DH = 64
CMP_BLOCK = 32
CMP_STRIDE = 16
CMP_HIDDEN = 64
SEL_BLOCK = 64
SEL_TOPK = 16
WINDOW = 512
FORCE_BONUS = 1000.0
MLA_HEADS = 8
MLA_Q_RANK = 256
MLA_KV_RANK = 128
MLA_NOPE = 64
MLA_ROPE = 32
MLA_V = 64
ROPE_THETA = 10000.0
N_GROUPS = 4
EXPERTS_PER_GROUP = 8
N_EXPERTS = N_GROUPS * EXPERTS_PER_GROUP
D_EXPERT = 256
MOE_BLOCK = 128
PLE_DIM = 256
Q_BLOCK = 128
LN_EPS = 1e-5
RMS_EPS = 1e-6
KVW = NSA_GROUPS * NSA_DH
LANES = 128
HEAD_LANES = 128
MLA_LANES = 256
NSA_SCALE = NSA_DH ** -0.5
MLA_SCALE = (MLA_NOPE + MLA_ROPE) ** -0.5
NEG_INF = float("-inf")

VMEM_LIMIT = 48 * 1024 * 1024


def _cparams(sem):
    return pltpu.CompilerParams(dimension_semantics=sem, vmem_limit_bytes=VMEM_LIMIT)


def _dot(a, b):
    return jnp.dot(a, b, preferred_element_type=F32)


def _dot_nt(a, b):
    return lax.dot_general(a, b, (((1,), (1,)), ((), ())), preferred_element_type=F32)


def _rms(x, g):
    return x * lax.rsqrt(jnp.mean(x * x, axis=-1, keepdims=True) + RMS_EPS) * g


def _ln(x, g, b):
    mu = jnp.mean(x, axis=-1, keepdims=True)
    var = jnp.mean(jnp.square(x - mu), axis=-1, keepdims=True)
    return (x - mu) * lax.rsqrt(var + LN_EPS) * g + b


_C_QW = 0
_C_NSA = _C_QW + NSA_HEADS * HEAD_LANES
_C_WIN = _C_NSA + 4 * KVW
_C_GT = _C_WIN + 2 * KVW
_C_DQ = _C_GT + LANES
_C_DKV = _C_DQ + MLA_Q_RANK
_C_KR = _C_DKV + MLA_KV_RANK
_C_KRS = _C_KR + LANES
_C_END = _C_KRS + LANES
_Q2_NOPE = MLA_HEADS * MLA_NOPE
_Q2_R1 = _Q2_NOPE
_Q2_R2 = _Q2_R1 + MLA_HEADS * LANES
_Q2_END = _Q2_R2 + MLA_HEADS * LANES


def _proj_kernel(x_ref, w1_ref, tab_ref, qn_ref, kvn_ref, wq2_ref, wuk_ref,
                 qw_ref, nsa_ref, cmp_ref, selbf_ref, win_ref, winbf_ref, gates_ref, qmla_ref, mla_ref, kmla_ref):
    z = _dot(x_ref[...].astype(BF), w1_ref[...])
    qw_ref[...] = (z[:, _C_QW:_C_NSA] * NSA_SCALE).astype(BF)
    nsa = z[:, _C_NSA:_C_WIN]
    nsa_ref[...] = nsa
    cmp_ref[...] = nsa[:, :2 * KVW]
    selbf_ref[...] = nsa[:, 2 * KVW:].astype(BF)
    win = z[:, _C_WIN:_C_GT]
    win_ref[...] = win
    winbf_ref[...] = win.astype(BF)
    gates_ref[...] = z[:, _C_GT:_C_DQ]
    rn = _rms(z[:, _C_DQ:_C_DKV], qn_ref[...])
    r = _dot(rn.astype(BF), wq2_ref[...])
    lat = _dot(r[:, :_Q2_NOPE].astype(BF), wuk_ref[...])
    tab = tab_ref[...]
    c1, c2 = tab[:, 0:LANES], tab[:, LANES:2 * LANES]
    ck, sk = tab[:, 2 * LANES:3 * LANES], tab[:, 3 * LANES:4 * LANES]
    for h in range(MLA_HEADS):
        base = h * MLA_LANES
        qmla_ref[:, base:base + LANES] = lat[:, h * LANES:(h + 1) * LANES].astype(BF)
        pe = (r[:, _Q2_R1 + h * LANES:_Q2_R1 + (h + 1) * LANES] * c1
              + r[:, _Q2_R2 + h * LANES:_Q2_R2 + (h + 1) * LANES] * c2)
        qmla_ref[:, base + LANES:base + MLA_LANES] = pe.astype(BF)
    ckv = _rms(z[:, _C_DKV:_C_KR], kvn_ref[...])
    kpe = z[:, _C_KR:_C_KRS] * ck + z[:, _C_KRS:_C_END] * sk
    mla_ref[:, 0:MLA_KV_RANK] = ckv
    mla_ref[:, MLA_KV_RANK:MLA_KV_RANK + MLA_ROPE] = kpe[:, 0:MLA_ROPE]
    kmla_ref[:, 0:LANES] = ckv.astype(BF)
    ones_lane = lax.broadcasted_iota(jnp.int32, kpe.shape, 1) == MLA_ROPE
    kmla_ref[:, LANES:MLA_LANES] = jnp.where(ones_lane, 1.0, kpe).astype(BF)


def _rope_tables(pos):
    half = MLA_ROPE // 2
    inv = ROPE_THETA ** (-jnp.arange(half, dtype=F32) / half)
    ang = pos.astype(F32)[:, None] * inv
    cos, sin = jnp.cos(ang), jnp.sin(ang)
    pad = jnp.zeros((pos.shape[0], LANES - MLA_ROPE), F32)
    return jnp.concatenate([cos, sin, pad, -sin, cos, pad, cos, cos, pad, -sin, sin, pad], axis=1)


def _arrange_proj_weights(w_in, w_uq, w_uk):
    d = w_in.shape[0]
    o = np.cumsum((0, NSA_HEADS * NSA_DH, KVW, KVW, KVW, KVW, KVW, KVW, 3 * NSA_HEADS,
                   MLA_Q_RANK, MLA_KV_RANK, MLA_ROPE, D_MODEL, D_MODEL))
    zeros = lambda n: jnp.zeros((d, n), w_in.dtype)
    cols = []
    for c in range(NSA_HEADS):
        wq = w_in[:, c * NSA_DH:(c + 1) * NSA_DH]
        cols += [wq, zeros(NSA_DH)] if c < NSA_HPG else [zeros(NSA_DH), wq]
    cols.append(w_in[:, o[1]:o[7]])
    cols += [w_in[:, o[7]:o[8]], zeros(LANES - 3 * NSA_HEADS)]
    cols.append(w_in[:, o[8]:o[10]])
    kr = w_in[:, o[10]:o[11]]
    half = MLA_ROPE // 2
    cols += [kr, zeros(LANES - MLA_ROPE), kr[:, half:], kr[:, :half], zeros(LANES - MLA_ROPE)]
    w1 = jnp.concatenate(cols, axis=1).astype(BF)
    w_ga = w_in[:, o[11]:o[12]].astype(BF)
    w_gb = w_in[:, o[12]:o[13]].astype(BF)

    uq = w_uq.reshape(MLA_Q_RANK, MLA_HEADS, MLA_NOPE + MLA_ROPE)
    zq = jnp.zeros((MLA_Q_RANK, LANES - MLA_ROPE), w_uq.dtype)
    q2 = [uq[:, :, :MLA_NOPE].reshape(MLA_Q_RANK, -1)]
    for lo in (MLA_NOPE, MLA_NOPE + half):
        for h in range(MLA_HEADS):
            x = uq[:, h, lo:lo + half]
            q2 += [x, x, zq]
    wq2 = jnp.concatenate(q2, axis=1).astype(BF)
    wuk = jnp.zeros((MLA_HEADS * MLA_NOPE, MLA_HEADS * LANES), w_uk.dtype)
    for h in range(MLA_HEADS):
        wuk = wuk.at[h * MLA_NOPE:(h + 1) * MLA_NOPE, h * LANES:(h + 1) * LANES].set(w_uk[:, h, :].T)
    return w1, w_ga, w_gb, wq2, wuk.astype(BF)


def _project(x2d, tab, tab_blocks, w1, qn, kvn, wq2, wuk, tm=256):
    n = x2d.shape[0]
    grid = (n // tm,)
    row = lambda w: pl.BlockSpec((tm, w), lambda i: (i, 0))
    full = lambda a: pl.BlockSpec(a.shape, lambda i: (0,) * a.ndim)
    outs = [(NSA_HEADS * HEAD_LANES, BF), (4 * KVW, F32), (2 * KVW, F32), (2 * KVW, BF), (2 * KVW, F32), (2 * KVW, BF),
            (LANES, F32), (MLA_HEADS * MLA_LANES, BF), (MLA_KV_RANK + MLA_ROPE, F32), (MLA_LANES, BF)]
    return pl.pallas_call(
        _proj_kernel,
        grid=grid,
        in_specs=[row(D_MODEL), full(w1), pl.BlockSpec((tm, 4 * LANES), lambda i: (i % tab_blocks, 0)),
                  full(qn), full(kvn), full(wq2), full(wuk)],
        out_specs=[row(w) for w, _ in outs],
        out_shape=[jax.ShapeDtypeStruct((n, w), dt) for w, dt in outs],
        compiler_params=_cparams(("parallel",)),
        name="proj",
    )(x2d, w1, tab, qn, kvn, wq2, wuk)


def _chunk_rows(ref, l, nchunk, tiles_per_row, tiles):
    return jnp.concatenate([ref[0, pl.ds(l * tiles_per_row + j, nchunk, stride=CMP_STRIDE * tiles_per_row), :]
                            for j in tiles], axis=1)


def _compress_partial(chunk_rows, pe_ref, w1_ref):
    a = b = None
    for l in range(CMP_STRIDE):
        xl = chunk_rows(l)
        ta = _dot((xl + pe_ref[l:l + 1, :]).astype(BF), w1_ref[l])
        tb = _dot((xl + pe_ref[CMP_STRIDE + l:CMP_STRIDE + l + 1, :]).astype(BF), w1_ref[CMP_STRIDE + l])
        a = ta if a is None else a + ta
        b = tb if b is None else b + tb
    return a, b


def _compress_kernel(x_ref, pe_ref, w1_ref, w2_ref, o_ref, b_scr):
    nchunk = o_ref.shape[1]
    a, b = _compress_partial(lambda l: _chunk_rows(x_ref, l, nchunk, 2, (0, 1)), pe_ref, w1_ref)
    b_scr[0:nchunk, :] = b
    b_scr[nchunk:nchunk + 8, :] = jnp.zeros((8, 4 * NSA_DH), F32)
    pre = a + b_scr[pl.ds(1, nchunk), :]
    o_ref[0] = _dot(jax.nn.gelu(pre).astype(BF), w2_ref[...]).astype(BF)


def _arrange_compress_weights(pe_k, pe_v, k_w1, k_w2, v_w1, v_w2):
    dh, hid = NSA_DH, CMP_HIDDEN
    pe = jnp.concatenate([pe_k, pe_k, pe_v, pe_v], axis=1)
    w1k = k_w1.reshape(CMP_BLOCK, dh, hid)
    w1v = v_w1.reshape(CMP_BLOCK, dh, hid)
    w1 = jnp.zeros((CMP_BLOCK, 4 * dh, 4 * hid), k_w1.dtype)
    w2 = jnp.zeros((4 * hid, 4 * dh), k_w2.dtype)
    for blk, (a1, a2) in enumerate(((w1k, k_w2), (w1k, k_w2), (w1v, v_w2), (w1v, v_w2))):
        w1 = w1.at[:, blk * dh:(blk + 1) * dh, blk * hid:(blk + 1) * hid].set(a1)
        w2 = w2.at[blk * hid:(blk + 1) * hid, blk * dh:(blk + 1) * dh].set(a2)
    return pe, w1.astype(BF), w2.astype(BF)


def _compress_prompt(cmp_rows, pe, w1, w2):
    b, s, _ = cmp_rows.shape
    nchunk = s // CMP_STRIDE
    full = lambda a: pl.BlockSpec(a.shape, lambda i: (0,) * a.ndim)
    cmp_rows = cmp_rows.reshape(b, 2 * s, LANES)
    return pl.pallas_call(
        _compress_kernel,
        grid=(b,),
        in_specs=[pl.BlockSpec((1, 2 * s, LANES), lambda i: (i, 0, 0)), full(pe), full(w1), full(w2)],
        out_specs=pl.BlockSpec((1, nchunk, 4 * NSA_DH), lambda i: (i, 0, 0)),
        out_shape=jax.ShapeDtypeStruct((b, nchunk, 4 * NSA_DH), BF),
        scratch_shapes=[pltpu.VMEM((nchunk + 8, 4 * NSA_DH), F32)],
        compiler_params=_cparams(("parallel",)),
        name="compress_prompt",
    )(cmp_rows, pe, w1, w2)


def _masked_softmax(s, mask):
    s = jnp.where(mask, s, NEG_INF)
    m = jnp.max(s, axis=-1, keepdims=True)
    m = jnp.where(m == NEG_INF, 0.0, m)
    e = jnp.exp(s - m)
    den = jnp.sum(e, axis=-1, keepdims=True)
    return e / jnp.where(den > 0, den, 1.0)


def _online_update(s, v, m_ref, l_ref, acc_ref, v_transposed=False):
    m_old = m_ref[...]
    m_new = jnp.maximum(m_old, jnp.max(s, axis=-1, keepdims=True))
    m_safe = jnp.where(m_new == NEG_INF, 0.0, m_new)
    alpha = jnp.exp(m_old - m_safe)
    e = jnp.exp(s - m_safe)
    l_ref[...] = alpha * l_ref[...] + jnp.sum(e, axis=-1, keepdims=True)
    pv = _dot_nt(e.astype(BF), v) if v_transposed else _dot(e.astype(BF), v)
    acc_ref[...] = alpha * acc_ref[...] + pv
    m_ref[...] = m_new


def _online_init(m_ref, l_ref, acc_ref):
    m_ref[...] = jnp.full(m_ref.shape, NEG_INF, F32)
    l_ref[...] = jnp.zeros(l_ref.shape, F32)
    acc_ref[...] = jnp.zeros(acc_ref.shape, F32)


def _online_result(l_ref, acc_ref):
    l = l_ref[...]
    return acc_ref[...] / jnp.where(l > 0, l, 1.0)


def _alibi_slope_rows(g, rows_per_head):
    hh = lax.broadcasted_iota(jnp.int32, (NSA_HPG * rows_per_head, 1), 0) // rows_per_head
    slope = jnp.zeros(hh.shape, F32)
    for h in range(NSA_HPG):
        slope = jnp.where(hh == h, 2.0 ** -(g * NSA_HPG + h + 1), slope)
    return slope


def _select_topk(score, n_lanes):
    j = lax.broadcasted_iota(jnp.int32, score.shape, 1)
    sel = jnp.zeros(score.shape, F32)
    for _ in range(SEL_TOPK):
        m = jnp.max(score, axis=-1, keepdims=True)
        idx = jnp.min(jnp.where(score == m, j, n_lanes), axis=-1, keepdims=True)
        pick = (j == idx) & (m > NEG_INF)
        sel = jnp.where(pick, 1.0, sel)
        score = jnp.where(pick, NEG_INF, score)
    return sel


def _split_dot(x, w):
    hi = x.astype(BF)
    lo = (x - hi.astype(F32)).astype(BF)
    return _dot(hi, w) + _dot(lo, w)


SEL_TILE = 512


def _nsa_prompt_kernel(qw_ref, gt_ref, kcvc_ref, sel_ref, win_ref, ovl_ref, o_ref, *scratch, nc):
    tq = qw_ref.shape[1]
    ncp = kcvc_ref.shape[1]
    q0 = pl.program_id(1) * tq
    rows = NSA_HPG * tq
    m_refs, l_refs, acc_refs = scratch[0:2], scratch[2:4], scratch[4:6]
    qi = lax.broadcasted_iota(jnp.int32, (rows, 1), 0) % tq
    q_pos = q0 + qi
    qp1 = q0 + lax.broadcasted_iota(jnp.int32, (tq, 1), 0)
    kc = kcvc_ref[0, :, 0:KVW]
    vc = kcvc_ref[0, :, KVW:2 * KVW]
    gates = jax.nn.sigmoid(gt_ref[0])
    wstart = pl.multiple_of(jnp.maximum(q0 - WINDOW, 0), tq)
    wk = WINDOW + tq
    kw = win_ref[0, pl.ds(wstart, wk), 0:KVW]
    vw = win_ref[0, pl.ds(wstart, wk), KVW:2 * KVW]
    jblk = lax.broadcasted_iota(jnp.int32, (tq, LANES), 1)
    c = lax.broadcasted_iota(jnp.int32, (1, ncp), 1)
    c_end = c * CMP_STRIDE + CMP_BLOCK - 1
    rel_c = (c_end - q0).astype(F32)
    wpos = wstart + lax.broadcasted_iota(jnp.int32, (1, wk), 1)
    rel_w = (wpos - q0).astype(F32)
    dist_w = q_pos - wpos

    qs, slopes, o_cs, o_ws, sels = [], [], [], [], []
    for g in range(NSA_GROUPS):
        q = jnp.concatenate([qw_ref[0, :, (g * NSA_HPG + h) * HEAD_LANES:(g * NSA_HPG + h + 1) * HEAD_LANES]
                             for h in range(NSA_HPG)], axis=0)
        slope = _alibi_slope_rows(g, tq)
        qs.append(q)
        slopes.append(slope)
        p_c = _masked_softmax(_dot_nt(q, kc) + slope * rel_c, (c_end <= q_pos) & (c < nc))
        o_cs.append(_dot(p_c.astype(BF), vc))
        p_sum = p_c[0:tq]
        for h in range(1, NSA_HPG):
            p_sum = p_sum + p_c[h * tq:(h + 1) * tq]
        imp = _split_dot(p_sum, ovl_ref[...])
        cur = qp1 // SEL_BLOCK
        avail = jblk * SEL_BLOCK <= qp1
        forced = (jblk == 0) | (jblk == cur) | (jblk == cur - 1)
        score = jnp.where(avail, imp + FORCE_BONUS * forced.astype(F32), NEG_INF)
        sels.append(_select_topk(score, LANES).astype(BF))
        p_w = _masked_softmax(_dot_nt(q, kw) + slope * rel_w, (dist_w >= 0) & (dist_w < WINDOW))
        o_ws.append(_dot(p_w.astype(BF), vw))
        _online_init(m_refs[g], l_refs[g], acc_refs[g])

    def tile(t, causal):
        k0 = pl.multiple_of(t * SEL_TILE, SEL_TILE)
        kk = sel_ref[0, pl.ds(k0, SEL_TILE), 0:KVW]
        vv = sel_ref[0, pl.ds(k0, SEL_TILE), KVW:2 * KVW]
        kpos = k0 + lax.broadcasted_iota(jnp.int32, (1, SEL_TILE), 1)
        rel = (kpos - q0).astype(F32)
        expand = lax.broadcasted_iota(jnp.int32, (LANES, SEL_TILE), 0) == kpos // SEL_BLOCK
        expand = jnp.where(expand, 1.0, 0.0).astype(BF)
        for g in range(NSA_GROUPS):
            picked = _dot(sels[g], expand) > 0.5
            s = (_dot_nt(qs[g], kk) + slopes[g] * rel).reshape(NSA_HPG, tq, SEL_TILE)
            s = jnp.where(picked[None], s, NEG_INF).reshape(rows, SEL_TILE)
            if causal:
                s = jnp.where(kpos <= q_pos, s, NEG_INF)
            _online_update(s, vv, m_refs[g], l_refs[g], acc_refs[g])

    def step(t, carry):
        tile(t, False)
        return carry

    n_full = q0 // SEL_TILE
    lax.fori_loop(0, n_full, step, 0)
    tile(n_full, True)
    for g in range(NSA_GROUPS):
        o_s = _online_result(l_refs[g], acc_refs[g])
        for h in range(NSA_HPG):
            head = g * NSA_HPG + h
            r = slice(h * tq, (h + 1) * tq)
            out = (gates[:, 3 * head:3 * head + 1] * o_cs[g][r]
                   + gates[:, 3 * head + 1:3 * head + 2] * o_s[r]
                   + gates[:, 3 * head + 2:3 * head + 3] * o_ws[g][r])
            o_ref[0, :, head * HEAD_LANES:(head + 1) * HEAD_LANES] = out.astype(BF)


def _overlap_matrix(ncp, nc):
    c0 = np.arange(ncp)[:, None] * CMP_STRIDE
    c1 = c0 + CMP_BLOCK - 1
    s0 = np.arange(LANES)[None, :] * SEL_BLOCK
    s1 = s0 + SEL_BLOCK - 1
    ok = (c0 <= s1) & (c1 >= s0) & (np.arange(ncp)[:, None] < nc)
    return jnp.asarray(ok.astype(np.float32)).astype(BF)


def _nsa_prompt(qw, gates, kcvc, sel_bf, win_bf):
    b, s, _ = qw.shape
    tq = Q_BLOCK
    ncp = kcvc.shape[1]
    nc = (s - CMP_BLOCK) // CMP_STRIDE + 1
    ovl = _overlap_matrix(ncp, nc)
    rows = NSA_HPG * tq
    return pl.pallas_call(
        functools.partial(_nsa_prompt_kernel, nc=nc),
        grid=(b, s // tq),
        in_specs=[pl.BlockSpec((1, tq, NSA_HEADS * HEAD_LANES), lambda bi, i: (bi, i, 0)),
                  pl.BlockSpec((1, tq, LANES), lambda bi, i: (bi, i, 0)),
                  pl.BlockSpec((1, ncp, 2 * KVW), lambda bi, i: (bi, 0, 0)),
                  pl.BlockSpec((1, s, 2 * KVW), lambda bi, i: (bi, 0, 0)),
                  pl.BlockSpec((1, s, 2 * KVW), lambda bi, i: (bi, 0, 0)),
                  pl.BlockSpec(ovl.shape, lambda bi, i: (0, 0))],
        out_specs=pl.BlockSpec((1, tq, NSA_HEADS * HEAD_LANES), lambda bi, i: (bi, i, 0)),
        out_shape=jax.ShapeDtypeStruct((b, s, NSA_HEADS * HEAD_LANES), BF),
        scratch_shapes=([pltpu.VMEM((rows, 1), F32)] * (2 * NSA_GROUPS) + [pltpu.VMEM((rows, KVW), F32)] * NSA_GROUPS),
        compiler_params=_cparams(("parallel", "arbitrary")),
        name="nsa_prompt",
    )(qw, gates, kcvc, sel_bf, win_bf, ovl)


MLA_TILE = 512


MLA_ONES_LANE = MLA_KV_RANK + MLA_ROPE


MLA_CHAINS = 4


def _mla_prompt_kernel(q_ref, k_ref, wuv_ref, o_ref, *scratch):
    tq = q_ref.shape[1]
    q0 = pl.program_id(1) * tq
    hpc = MLA_HEADS // MLA_CHAINS
    rows = hpc * tq
    m_refs, acc_refs = scratch[:MLA_CHAINS], scratch[MLA_CHAINS:]
    qs = [jnp.concatenate([q_ref[0, :, h * MLA_LANES:(h + 1) * MLA_LANES] for h in range(c * hpc, (c + 1) * hpc)],
                          axis=0) for c in range(MLA_CHAINS)]
    q_pos = q0 + lax.broadcasted_iota(jnp.int32, (rows, 1), 0) % tq
    for c in range(MLA_CHAINS):
        m_refs[c][...] = jnp.full((rows, 1), NEG_INF, F32)
        acc_refs[c][...] = jnp.zeros((rows, MLA_LANES), F32)

    def tile(t, causal):
        k0 = pl.multiple_of(t * MLA_TILE, MLA_TILE)
        kk = k_ref[0, pl.ds(k0, MLA_TILE), :]
        for c in range(MLA_CHAINS):
            s = _dot_nt(qs[c], kk) * MLA_SCALE
            if causal:
                kpos = k0 + lax.broadcasted_iota(jnp.int32, (1, MLA_TILE), 1)
                s = jnp.where(kpos <= q_pos, s, NEG_INF)
            m_old = m_refs[c][...]
            m_new = jnp.maximum(m_old, jnp.max(s, axis=-1, keepdims=True))
            m_refs[c][...] = m_new
            e = jnp.exp(s - m_new)
            acc_refs[c][...] = jnp.exp(m_old - m_new) * acc_refs[c][...] + _dot(e.astype(BF), kk)

    def step(t, carry):
        tile(t, False)
        return carry

    n_full = q0 // MLA_TILE
    lax.fori_loop(0, n_full, step, 0)
    tile(n_full, True)
    out = None
    for c in range(MLA_CHAINS):
        acc = acc_refs[c][...]
        o_lat = (acc[:, 0:MLA_KV_RANK] / acc[:, MLA_ONES_LANE:MLA_ONES_LANE + 1]).astype(BF)
        for i in range(hpc):
            part = _dot(o_lat[i * tq:(i + 1) * tq], wuv_ref[c * hpc + i])
            out = part if out is None else out + part
    o_ref[0] = out.astype(BF)


def _arrange_wuv(w_uv):
    w = jnp.zeros((MLA_HEADS, MLA_KV_RANK, MLA_HEADS * MLA_V), w_uv.dtype)
    for h in range(MLA_HEADS):
        w = w.at[h, :, h * MLA_V:(h + 1) * MLA_V].set(w_uv[:, h, :])
    return w.astype(BF)


def _mla_prompt(qmla, kmla, wuv):
    b, s, _ = qmla.shape
    tq = Q_BLOCK
    rows = MLA_HEADS * tq
    return pl.pallas_call(
        _mla_prompt_kernel,
        grid=(b, s // tq),
        in_specs=[pl.BlockSpec((1, tq, MLA_HEADS * MLA_LANES), lambda bi, i: (bi, i, 0)),
                  pl.BlockSpec((1, s, MLA_LANES), lambda bi, i: (bi, 0, 0)),
                  pl.BlockSpec(wuv.shape, lambda bi, i: (0, 0, 0))],
        out_specs=pl.BlockSpec((1, tq, MLA_HEADS * MLA_V), lambda bi, i: (bi, i, 0)),
        out_shape=jax.ShapeDtypeStruct((b, s, MLA_HEADS * MLA_V), BF),
        scratch_shapes=([pltpu.VMEM((rows // MLA_CHAINS, 1), F32)] * MLA_CHAINS
                        + [pltpu.VMEM((rows // MLA_CHAINS, MLA_LANES), F32)] * MLA_CHAINS),
        compiler_params=_cparams(("parallel", "arbitrary")),
        name="mla_prompt",
    )(qmla, kmla, wuv)


_R_EXPERT0 = 32


def _merge_kernel(x_ref, on_ref, om_ref, wga_ref, wgb_ref, wa_ref, wb_ref, wo_ref, g1_ref, b1_ref,
                  wr_ref, br_ref, h_ref, route_ref, *, alpha):
    x = x_ref[...]
    xb = x.astype(BF)
    ga = jax.nn.sigmoid(_dot(xb, wga_ref[...]))
    gb = jax.nn.sigmoid(_dot(xb, wgb_ref[...]))
    mix = ga * _dot(on_ref[...], wa_ref[...]) + gb * _dot(om_ref[...], wb_ref[...])
    h = _ln(alpha * x + _dot(mix.astype(BF), wo_ref[...]), g1_ref[...], b1_ref[...])
    h_ref[...] = h
    logit = _dot(h.astype(BF), wr_ref[...]) + br_ref[...]
    lane = lax.broadcasted_iota(jnp.int32, logit.shape, 1)
    is_g = lane < N_GROUPS
    gl = jnp.where(is_g, logit, NEG_INF)
    gmax = jnp.max(gl, axis=-1, keepdims=True)
    gsel = jnp.min(jnp.where(gl == gmax, lane, LANES), axis=-1, keepdims=True)
    g_w = 1.0 / jnp.sum(jnp.exp(gl - gmax), axis=-1, keepdims=True)
    e_lane = lane - _R_EXPERT0
    in_grp = (e_lane >= 0) & (e_lane < N_EXPERTS) & (e_lane // EXPERTS_PER_GROUP == gsel)
    el = jnp.where(in_grp, logit, NEG_INF)
    v1 = jnp.max(el, axis=-1, keepdims=True)
    i1 = jnp.min(jnp.where(el == v1, e_lane, LANES), axis=-1, keepdims=True)
    el2 = jnp.where(e_lane == i1, NEG_INF, el)
    v2 = jnp.max(el2, axis=-1, keepdims=True)
    i2 = jnp.min(jnp.where(el2 == v2, e_lane, LANES), axis=-1, keepdims=True)
    e2 = jnp.exp(v2 - v1)
    den = 1.0 + e2
    route = jnp.where(lane == 0, i1.astype(F32),
                      jnp.where(lane == 1, i2.astype(F32),
                                jnp.where(lane == 2, g_w * (1.0 / den),
                                          jnp.where(lane == 3, g_w * (e2 / den), 0.0))))
    route_ref[...] = route


def _arrange_merge_weights(w_up_a, w_group, b_group, w_router, b_router):
    wa = jnp.zeros((NSA_HEADS * HEAD_LANES, D_MODEL), w_up_a.dtype)
    for c in range(NSA_HEADS):
        off = c * HEAD_LANES + (0 if c < NSA_HPG else NSA_DH)
        wa = wa.at[off:off + NSA_DH].set(w_up_a[c * NSA_DH:(c + 1) * NSA_DH])
    d = w_group.shape[0]
    wr = jnp.concatenate([w_group, jnp.zeros((d, _R_EXPERT0 - N_GROUPS), w_group.dtype), w_router,
                          jnp.zeros((d, LANES - _R_EXPERT0 - N_EXPERTS), w_group.dtype)], axis=1)
    br = jnp.concatenate([b_group, jnp.zeros((_R_EXPERT0 - N_GROUPS,), b_group.dtype), b_router,
                          jnp.zeros((LANES - _R_EXPERT0 - N_EXPERTS,), b_group.dtype)])[None]
    return wa.astype(BF), wr.astype(BF), br


def _merge(x2d, on, om, w_ga, w_gb, wa, wb, wo, g1, b1, wr, br, alpha, tm=256):
    n = x2d.shape[0]
    row = lambda w: pl.BlockSpec((tm, w), lambda i: (i, 0))
    full = lambda a: pl.BlockSpec(a.shape, lambda i: (0,) * a.ndim)
    consts = (w_ga, w_gb, wa, wb, wo, g1, b1, wr, br)
    return pl.pallas_call(
        functools.partial(_merge_kernel, alpha=alpha),
        grid=(n // tm,),
        in_specs=[row(D_MODEL), row(on.shape[1]), row(om.shape[1])] + [full(a) for a in consts],
        out_specs=[row(D_MODEL), row(LANES)],
        out_shape=[jax.ShapeDtypeStruct((n, D_MODEL), F32), jax.ShapeDtypeStruct((n, LANES), F32)],
        compiler_params=_cparams(("parallel",)),
        name="merge_ln1_route",
    )(x2d, on, om, *consts)


def _expert_kernel(blk_e_ref, src_ref, dst_ref, h_hbm, wg_ref, wu_ref, wd_ref, y_hbm, xbuf, ybuf, gsem, ssem):
    i = pl.program_id(0)
    nb = pl.num_programs(0)
    slot = i % 2
    d = xbuf.shape[2]

    def gather(blk, sl, r):
        return pltpu.make_async_copy(h_hbm.at[pl.ds(src_ref[blk * MOE_BLOCK + r], 1)], xbuf.at[sl, pl.ds(r, 1)],
                                     gsem.at[sl])

    def scatter(blk, sl, r):
        dst = dst_ref[blk * MOE_BLOCK + r]
        col = pl.multiple_of((dst % 2) * d, d)
        return pltpu.make_async_copy(ybuf.at[sl, pl.ds(r, 1)], y_hbm.at[pl.ds(dst // 2, 1), pl.ds(col, d)],
                                     ssem.at[sl])

    def for_rows(fn):
        def body(r, c):
            fn(r)
            return c
        lax.fori_loop(0, MOE_BLOCK, body, 0, unroll=8)

    @pl.when(i == 0)
    def _():
        for_rows(lambda r: gather(i, slot, r).start())

    @pl.when(i + 1 < nb)
    def _():
        for_rows(lambda r: gather(i + 1, 1 - slot, r).start())

    for_rows(lambda r: gather(i, slot, r).wait())

    @pl.when(i >= 2)
    def _():
        for_rows(lambda r: scatter(i - 2, slot, r).wait())

    xb = xbuf[slot].astype(BF)
    hid = jax.nn.silu(_dot(xb, wg_ref[0])) * _dot(xb, wu_ref[0])
    ybuf[slot] = _dot(hid.astype(BF), wd_ref[0])
    for_rows(lambda r: scatter(i, slot, r).start())

    @pl.when(i == nb - 1)
    def _():
        for_rows(lambda r: scatter(i - 1, 1 - slot, r).wait())
        for_rows(lambda r: scatter(i, slot, r).wait())


def _experts(h, blk_e, src, dst, wg, wu, wd):
    n, d = h.shape
    nb = blk_e.shape[0]
    assert nb >= 2
    wspec = lambda a: pl.BlockSpec((1,) + a.shape[1:], lambda i, e, s, t: (e[i], 0, 0))
    return pl.pallas_call(
        _expert_kernel,
        grid_spec=pltpu.PrefetchScalarGridSpec(
            num_scalar_prefetch=3,
            grid=(nb,),
            in_specs=[pl.BlockSpec(memory_space=pl.ANY), wspec(wg), wspec(wu), wspec(wd)],
            out_specs=pl.BlockSpec(memory_space=pl.ANY),
            scratch_shapes=[pltpu.VMEM((2, MOE_BLOCK, d), F32), pltpu.VMEM((2, MOE_BLOCK, d), F32),
                            pltpu.SemaphoreType.DMA((2,)), pltpu.SemaphoreType.DMA((2,))]),
        out_shape=jax.ShapeDtypeStruct((n + MOE_BLOCK, 2 * d), F32),
        compiler_params=_cparams(("arbitrary",)),
        name="experts",
    )(blk_e, src, dst, h, wg, wu, wd)


def _route_plan(route, n):
    eid = route[:, 0:2].astype(jnp.int32).reshape(-1)
    m = eid.shape[0]
    order = jnp.argsort(eid)
    e_sorted = eid[order]
    counts = jnp.bincount(eid, length=N_EXPERTS)
    padded = (counts + MOE_BLOCK - 1) // MOE_BLOCK * MOE_BLOCK
    pad_end = jnp.cumsum(padded)
    start = jnp.cumsum(counts) - counts
    slot = (pad_end - padded)[e_sorted] + jnp.arange(m) - start[e_sorted]
    nb = -(-m // MOE_BLOCK) + N_EXPERTS
    src = jnp.zeros((nb * MOE_BLOCK,), jnp.int32).at[slot].set((order // 2).astype(jnp.int32))
    s_all = jnp.arange(nb * MOE_BLOCK, dtype=jnp.int32)
    dump = m + MOE_BLOCK * ((s_all // MOE_BLOCK) % 2) + s_all % MOE_BLOCK
    dst = dump.at[slot].set(order.astype(jnp.int32))
    blk_e = jnp.minimum(jnp.sum(pad_end[None, :] <= (jnp.arange(nb) * MOE_BLOCK)[:, None], axis=1),
                        N_EXPERTS - 1).astype(jnp.int32)
    return blk_e, src, dst


def _final_kernel(h_ref, y_ref, route_ref, g2_ref, b2_ref, wpg_ref, p_ref, wpe_ref, o_ref, *, alpha):
    h = h_ref[...]
    route = route_ref[...]
    moe = route[:, 2:3] * y_ref[:, 0:D_MODEL] + route[:, 3:4] * y_ref[:, D_MODEL:2 * D_MODEL]
    h2 = _ln(alpha * h + moe, g2_ref[...], b2_ref[...])
    gate = jax.nn.sigmoid(_dot(h2.astype(BF), wpg_ref[...]))
    o_ref[...] = h2 + gate * _dot(p_ref[...].astype(BF), wpe_ref[...])


def _final(h, y2, route, g2, b2, wpg, p2d, wpe, alpha, tm=256):
    n = h.shape[0]
    row = lambda w: pl.BlockSpec((tm, w), lambda i: (i, 0))
    full = lambda a: pl.BlockSpec(a.shape, lambda i: (0,) * a.ndim)
    return pl.pallas_call(
        functools.partial(_final_kernel, alpha=alpha),
        grid=(n // tm,),
        in_specs=[row(D_MODEL), row(2 * D_MODEL), row(LANES), full(g2), full(b2), full(wpg), row(PLE_DIM), full(wpe)],
        out_specs=row(D_MODEL),
        out_shape=jax.ShapeDtypeStruct((n, D_MODEL), F32),
        compiler_params=_cparams(("parallel",)),
        name="combine_ln2_ple",
    )(h, y2, route, g2, b2, wpg, p2d, wpe)


def _layer_tail(x2d, on, om, p2d, tw, alpha, tm=256):
    n = x2d.shape[0]
    h, route = _merge(x2d, on, om, tw["w_ga"], tw["w_gb"], tw["wa"], tw["wb"], tw["wo"], tw["g1"], tw["b1"],
                      tw["wr"], tw["br"], alpha, tm=tm)
    blk_e, src, dst = _route_plan(route, n)
    y = _experts(h, blk_e, src, dst, tw["wg"], tw["wu"], tw["wd"])
    return _final(h, y, route, tw["g2"], tw["b2"], tw["wpg"], p2d, tw["wpe"], alpha, tm=tm)


SAMPLE_PAGES_PER_STEP = 16
CMP_PAGES_PER_STEP = 64


def _stack_heads(ref, n_heads, lanes):
    return jnp.concatenate([ref[0, :, c * lanes:(c + 1) * lanes] for c in range(n_heads)], axis=0)


def _alibi_rows(first_head, n_heads, rows_per_head):
    hh = lax.broadcasted_iota(jnp.int32, (n_heads * rows_per_head, 1), 0) // rows_per_head
    slope = jnp.zeros(hh.shape, F32)
    for h in range(n_heads):
        slope = jnp.where(hh == h, 2.0 ** -(first_head + h + 1), slope)
    return slope


def _pad_rows(x, n):
    return jnp.concatenate([x, jnp.zeros((n - x.shape[0], x.shape[1]), x.dtype)], axis=0)


def _nsa_sample_cmp_kernel(pt_ref, qw_ref, pe_ref, w1_ref, w2_ref, ovl_ref, *rest, pps, past, nc):
    pages = rest[:pps]
    oc_ref, sel_ref, a_scr, b_scr, xk_scr, xv_scr = rest[pps:]
    step = pl.program_id(1)
    page = pages[0].shape[2]
    nstep = pps * page // CMP_STRIDE
    for k, pg in enumerate(pages):
        xk_scr[k * page:(k + 1) * page, :] = pg[0, 0:KVW, :].T
        xv_scr[k * page:(k + 1) * page, :] = pg[0, KVW:2 * KVW, :].T
    a, b = _compress_partial(
        lambda l: jnp.concatenate([xk_scr[pl.ds(l, nstep, stride=CMP_STRIDE), :],
                                   xv_scr[pl.ds(l, nstep, stride=CMP_STRIDE), :]], axis=1), pe_ref, w1_ref)
    r0 = pl.multiple_of(step * nstep, nstep)
    a_scr[pl.ds(r0, nstep), :] = a
    b_scr[pl.ds(r0, nstep), :] = b

    @pl.when(step == pl.num_programs(1) - 1)
    def _():
        _nsa_sample_select(qw_ref, w2_ref, ovl_ref, oc_ref, sel_ref, a_scr, b_scr, past=past, nc=nc)


def _nsa_sample_select(qw_ref, w2_ref, ovl_ref, oc_ref, sel_ref, a_scr, b_scr, *, past, nc):
    dec = qw_ref.shape[1]
    nchunk = a_scr.shape[0]
    nsl = sel_ref.shape[2]
    b_scr[nchunk:nchunk + 8, :] = jnp.zeros((8, 4 * NSA_DH), F32)
    pre = a_scr[...] + b_scr[pl.ds(1, nchunk), :]
    kcvc = _dot(jax.nn.gelu(pre).astype(BF), w2_ref[...]).astype(BF)
    kc = kcvc[:, 0:KVW]
    vc = kcvc[:, KVW:2 * KVW]
    rows = NSA_HPG * dec
    q_pos = past + lax.broadcasted_iota(jnp.int32, (rows, 1), 0) % dec
    qp1 = past + lax.broadcasted_iota(jnp.int32, (dec, 1), 0)
    c = lax.broadcasted_iota(jnp.int32, (1, nchunk), 1)
    jblk = lax.broadcasted_iota(jnp.int32, (dec, nsl), 1)
    for g in range(NSA_GROUPS):
        q = jnp.concatenate([qw_ref[0, :, (g * NSA_HPG + h) * HEAD_LANES:(g * NSA_HPG + h + 1) * HEAD_LANES]
                             for h in range(NSA_HPG)], axis=0)
        slope = _alibi_rows(g * NSA_HPG, NSA_HPG, dec)
        dist_c = q_pos - (c * CMP_STRIDE + CMP_BLOCK - 1)
        s_c = _dot_nt(q, kc) - slope * dist_c.astype(F32)
        p_c = _masked_softmax(s_c, (dist_c >= 0) & (c < nc))
        oc_ref[0, g * rows:(g + 1) * rows, :] = _dot(p_c.astype(BF), vc)
        p_sum = p_c[0:dec]
        for h in range(1, NSA_HPG):
            p_sum = p_sum + p_c[h * dec:(h + 1) * dec]
        imp = _split_dot(p_sum, ovl_ref[...])
        cur = qp1 // SEL_BLOCK
        avail = jblk * SEL_BLOCK <= qp1
        forced = (jblk == 0) | (jblk == cur) | (jblk == cur - 1)
        score = jnp.where(avail, imp + FORCE_BONUS * forced.astype(F32), NEG_INF)
        sel = _select_topk(score, nsl)
        sel_ref[0, g * rows:(g + 1) * rows, :] = jnp.concatenate([sel] * NSA_HPG, axis=0)


def _overlap_matrix_wide(ncp, nc, nsl):
    c0 = np.arange(ncp)[:, None] * CMP_STRIDE
    c1 = c0 + CMP_BLOCK - 1
    s0 = np.arange(nsl)[None, :] * SEL_BLOCK
    s1 = s0 + SEL_BLOCK - 1
    ok = (c0 <= s1) & (c1 >= s0) & (np.arange(ncp)[:, None] < nc)
    return jnp.asarray(ok.astype(np.float32)).astype(BF)


def _nsa_sample_cmp(page_table, pool, qw, pe, w1, w2, past):
    db, n_pages = page_table.shape
    page = pool.shape[2]
    dec = qw.shape[1]
    nchunk = n_pages * page // CMP_STRIDE
    nc = (past + dec - CMP_BLOCK) // CMP_STRIDE + 1
    ns = -(-(past + dec) // SEL_BLOCK)
    nsl = -(-ns // LANES) * LANES
    ovl = _overlap_matrix_wide(nchunk, nc, nsl)
    rows = NSA_HEADS * dec
    pps = min(CMP_PAGES_PER_STEP, n_pages)
    assert n_pages % pps == 0
    full = lambda a: pl.BlockSpec(a.shape, lambda s, j, pt: (0,) * a.ndim)
    page_spec = lambda k: pl.BlockSpec((1, 2 * KVW, page), lambda s, j, pt: (pt[s, j * pps + k], 0, 0))
    return pl.pallas_call(
        functools.partial(_nsa_sample_cmp_kernel, pps=pps, past=past, nc=nc),
        grid_spec=pltpu.PrefetchScalarGridSpec(
            num_scalar_prefetch=1,
            grid=(db, n_pages // pps),
            in_specs=[pl.BlockSpec((1, dec, NSA_HEADS * HEAD_LANES), lambda s, j, pt: (s, 0, 0)),
                      full(pe), full(w1), full(w2), full(ovl)] + [page_spec(k) for k in range(pps)],
            out_specs=[pl.BlockSpec((1, rows, KVW), lambda s, j, pt: (s, 0, 0)),
                       pl.BlockSpec((1, rows, nsl), lambda s, j, pt: (s, 0, 0))],
            scratch_shapes=[pltpu.VMEM((nchunk, 4 * NSA_DH), F32), pltpu.VMEM((nchunk + 8, 4 * NSA_DH), F32),
                            pltpu.VMEM((pps * page, KVW), F32), pltpu.VMEM((pps * page, KVW), F32)]),
        out_shape=[jax.ShapeDtypeStruct((db, rows, KVW), F32), jax.ShapeDtypeStruct((db, rows, nsl), F32)],
        compiler_params=pltpu.CompilerParams(dimension_semantics=("parallel", "arbitrary"),
                                             vmem_limit_bytes=56 * 1024 * 1024),
        name="nsa_sample_compress_select",
    )(page_table, qw, pe, w1, w2, ovl, *([pool] * pps))


def _nsa_sample_sel_kernel(pt_ref, qw_ref, gt_ref, oc_ref, selt_ref, self_ref, new_ref, winold_ref, winnew_ref,
                           *rest, pps, past):
    pages = rest[:pps]
    o_ref, m_ref, l_ref, acc_ref = rest[pps:]
    step = pl.program_id(1)
    dec = qw_ref.shape[1]
    page = pages[0].shape[2]
    rows = NSA_HEADS * dec
    nk = pps * page

    @pl.when(step == 0)
    def _():
        _online_init(m_ref, l_ref, acc_ref)

    q = _stack_heads(qw_ref, NSA_HEADS, HEAD_LANES)
    slope = _alibi_rows(0, NSA_HEADS, dec)
    q_pos = past + lax.broadcasted_iota(jnp.int32, (rows, 1), 0) % dec
    kt = jnp.concatenate([pg[0, 0:KVW, :] for pg in pages], axis=1).astype(BF)
    vt = jnp.concatenate([pg[0, KVW:2 * KVW, :] for pg in pages], axis=1).astype(BF)
    key = lax.broadcasted_iota(jnp.int32, (1, nk), 1)
    kpos = step * nk + key
    blocks_per_step = nk // SEL_BLOCK
    tile_off = (step % (LANES // blocks_per_step)) * blocks_per_step
    expand = lax.broadcasted_iota(jnp.int32, (LANES, nk), 0) == tile_off + key // SEL_BLOCK
    picked = _dot(selt_ref[0].astype(BF), jnp.where(expand, 1.0, 0.0).astype(BF))
    dist = q_pos - kpos
    s = _dot(q, kt) - slope * dist.astype(F32)
    _online_update(jnp.where((picked > 0.5) & (dist >= 0), s, NEG_INF), vt, m_ref, l_ref, acc_ref,
                   v_transposed=True)

    @pl.when(step == pl.num_programs(1) - 1)
    def _():
        inew = lax.broadcasted_iota(jnp.int32, (1, LANES), 1)
        dist_n = q_pos - (past + inew)
        valid_n = (inew < dec) & (dist_n >= 0)
        new = _pad_rows(new_ref[0], LANES)
        blk_new = past // SEL_BLOCK
        picked_n = self_ref[0][:, blk_new:blk_new + 1]
        s_n = _dot_nt(q, new[:, 0:KVW].astype(BF)) - slope * dist_n.astype(F32)
        _online_update(jnp.where(valid_n & (picked_n > 0.5), s_n, NEG_INF), new[:, KVW:2 * KVW].astype(BF),
                       m_ref, l_ref, acc_ref)
        o_s = _online_result(l_ref, acc_ref)
        wb = winold_ref.shape[2]
        wpos = past - wb + lax.broadcasted_iota(jnp.int32, (1, wb), 1)
        dist_o = q_pos - wpos
        s_o = _dot(q, winold_ref[0, 0:KVW, :].astype(BF)) - slope * dist_o.astype(F32)
        s_o = jnp.where((dist_o >= 0) & (dist_o < WINDOW) & (wpos >= 0), s_o, NEG_INF)
        wnew = _pad_rows(winnew_ref[0], LANES)
        s_w = _dot_nt(q, wnew[:, 0:KVW].astype(BF)) - slope * dist_n.astype(F32)
        s_w = jnp.where(valid_n & (dist_n < WINDOW), s_w, NEG_INF)
        mw = jnp.maximum(jnp.max(s_o, axis=-1, keepdims=True), jnp.max(s_w, axis=-1, keepdims=True))
        mw = jnp.where(mw == NEG_INF, 0.0, mw)
        e_o = jnp.exp(s_o - mw)
        e_w = jnp.exp(s_w - mw)
        den = jnp.sum(e_o, axis=-1, keepdims=True) + jnp.sum(e_w, axis=-1, keepdims=True)
        o_w = (_dot_nt(e_o.astype(BF), winold_ref[0, KVW:2 * KVW, :].astype(BF))
               + _dot(e_w.astype(BF), wnew[:, KVW:2 * KVW].astype(BF))) / jnp.where(den > 0, den, 1.0)
        gates = jax.nn.sigmoid(gt_ref[0])
        o_c = oc_ref[0]
        for head in range(NSA_HEADS):
            r = slice(head * dec, (head + 1) * dec)
            out = (gates[:, 3 * head:3 * head + 1] * o_c[r] + gates[:, 3 * head + 1:3 * head + 2] * o_s[r]
                   + gates[:, 3 * head + 2:3 * head + 3] * o_w[r])
            o_ref[0, :, head * HEAD_LANES:(head + 1) * HEAD_LANES] = out.astype(BF)


def _nsa_sample_sel(page_table, pool, qw, gates, o_c, sel, new_rows, win_old, win_new, past):
    db, n_pages = page_table.shape
    page = pool.shape[2]
    dec = qw.shape[1]
    pps = min(SAMPLE_PAGES_PER_STEP, n_pages)
    assert n_pages % pps == 0
    rows = NSA_HEADS * dec
    nsl = sel.shape[2]
    steps_per_tile = LANES // (pps * page // SEL_BLOCK)
    blk3 = lambda a: pl.BlockSpec((1,) + a.shape[1:], lambda s, j, pt: (s, 0, 0))
    page_spec = lambda k: pl.BlockSpec((1, 2 * KVW, page), lambda s, j, pt: (pt[s, j * pps + k], 1, 0))
    return pl.pallas_call(
        functools.partial(_nsa_sample_sel_kernel, pps=pps, past=past),
        grid_spec=pltpu.PrefetchScalarGridSpec(
            num_scalar_prefetch=1,
            grid=(db, n_pages // pps),
            in_specs=[blk3(qw), blk3(gates), blk3(o_c),
                      pl.BlockSpec((1, rows, LANES), lambda s, j, pt: (s, 0, j // steps_per_tile)),
                      blk3(sel), blk3(new_rows), blk3(win_old), blk3(win_new)] + [page_spec(k) for k in range(pps)],
            out_specs=pl.BlockSpec((1, dec, NSA_HEADS * HEAD_LANES), lambda s, j, pt: (s, 0, 0)),
            scratch_shapes=[pltpu.VMEM((rows, 1), F32), pltpu.VMEM((rows, 1), F32), pltpu.VMEM((rows, KVW), F32)]),
        out_shape=jax.ShapeDtypeStruct((db, dec, NSA_HEADS * HEAD_LANES), BF),
        compiler_params=_cparams(("parallel", "arbitrary")),
        name="nsa_sample_select_window",
    )(page_table, qw, gates, o_c, sel, sel, new_rows, win_old, win_new, *([pool] * pps))


def _mla_sample_kernel(pt_ref, q_ref, new_ref, wuv_ref, *rest, pps):
    pages = rest[:pps]
    o_ref, m_ref, l_ref, acc_ref = rest[pps:]
    step = pl.program_id(1)
    dec = q_ref.shape[1]
    kvw = MLA_KV_RANK + MLA_ROPE

    @pl.when(step == 0)
    def _():
        _online_init(m_ref, l_ref, acc_ref)

    q = _stack_heads(q_ref, MLA_HEADS, MLA_LANES)[:, 0:kvw]
    kt = jnp.concatenate([pg[0] for pg in pages], axis=1).astype(BF)
    _online_update(_dot(q, kt) * MLA_SCALE, kt, m_ref, l_ref, acc_ref, v_transposed=True)

    @pl.when(step == pl.num_programs(1) - 1)
    def _():
        rows = MLA_HEADS * dec
        qi = lax.broadcasted_iota(jnp.int32, (rows, 1), 0) % dec
        inew = lax.broadcasted_iota(jnp.int32, (1, LANES), 1)
        new = _pad_rows(new_ref[0], LANES).astype(BF)
        s_n = jnp.where((inew < dec) & (inew <= qi), _dot_nt(q, new) * MLA_SCALE, NEG_INF)
        _online_update(s_n, new, m_ref, l_ref, acc_ref)
        o_lat = _online_result(l_ref, acc_ref)[:, 0:MLA_KV_RANK].astype(BF)
        out = _dot(o_lat[0:dec], wuv_ref[0])
        for h in range(1, MLA_HEADS):
            out = out + _dot(o_lat[h * dec:(h + 1) * dec], wuv_ref[h])
        o_ref[0] = out.astype(BF)


def _mla_sample(page_table, pool, qmla, new_rows, wuv):
    db, n_pages = page_table.shape
    kvw, page = pool.shape[1:]
    dec = qmla.shape[1]
    pps = min(SAMPLE_PAGES_PER_STEP, n_pages)
    assert n_pages % pps == 0
    rows = MLA_HEADS * dec
    blk3 = lambda a: pl.BlockSpec((1,) + a.shape[1:], lambda s, j, pt: (s, 0, 0))
    page_spec = lambda k: pl.BlockSpec((1, kvw, page), lambda s, j, pt: (pt[s, j * pps + k], 0, 0))
    return pl.pallas_call(
        functools.partial(_mla_sample_kernel, pps=pps),
        grid_spec=pltpu.PrefetchScalarGridSpec(
            num_scalar_prefetch=1,
            grid=(db, n_pages // pps),
            in_specs=[blk3(qmla), blk3(new_rows), pl.BlockSpec(wuv.shape, lambda s, j, pt: (0, 0, 0))]
                     + [page_spec(k) for k in range(pps)],
            out_specs=pl.BlockSpec((1, dec, MLA_HEADS * MLA_V), lambda s, j, pt: (s, 0, 0)),
            scratch_shapes=[pltpu.VMEM((rows, 1), F32), pltpu.VMEM((rows, 1), F32), pltpu.VMEM((rows, kvw), F32)]),
        out_shape=jax.ShapeDtypeStruct((db, dec, MLA_HEADS * MLA_V), BF),
        compiler_params=_cparams(("parallel", "arbitrary")),
        name="mla_sample",
    )(page_table, qmla, new_rows, wuv, *([pool] * pps))


def _tail_weights(w_in_parts, w_up_a, w_up_b, w_o, ln1_g, ln1_b, w_group, b_group, w_router, b_router,
                  exp_w_gate, exp_w_up, exp_w_down, ln2_g, ln2_b, w_pe, w_pg):
    w_ga, w_gb = w_in_parts
    wa, wr, br = _arrange_merge_weights(w_up_a, w_group, b_group, w_router, b_router)
    return dict(w_ga=w_ga, w_gb=w_gb, wa=wa, wb=w_up_b.astype(BF), wo=w_o.astype(BF), g1=ln1_g[None], b1=ln1_b[None],
                wr=wr, br=br, wg=exp_w_gate.astype(BF), wu=exp_w_up.astype(BF), wd=exp_w_down.astype(BF),
                g2=ln2_g[None], b2=ln2_b[None], wpg=w_pg.astype(BF), wpe=w_pe.astype(BF))


def kernel(x_prompt, x_sample, cache_nsa_kv, cache_mla, cache_nsa_win, page_table, p_prompt, p_sample,
           w_in, cmp_pe_k, cmp_pe_v, cmp_k_w1, cmp_k_w2, cmp_v_w1, cmp_v_w2,
           mla_q_norm, mla_kv_norm, mla_w_uq, mla_w_uk, mla_w_uv,
           w_up_a, w_up_b, w_o, ln1_g, ln1_b,
           w_group, b_group, w_router, b_router, exp_w_gate, exp_w_up, exp_w_down,
           ln2_g, ln2_b, w_pe, w_pg):
    depth = w_in.shape[0]
    b, s, d = x_prompt.shape
    db, dec, _ = x_sample.shape
    n_pool, page = cache_nsa_kv.shape[1:3]
    past = page_table.shape[1] * page
    wb = cache_nsa_win.shape[2]
    alpha = (2.0 * depth) ** 0.25
    np_, ns_ = b * s, db * dec
    tm_p, tm_s = min(256, np_), min(256, ns_)
    tab_p = _rope_tables(jnp.arange(s, dtype=jnp.int32))
    tab_s = jnp.tile(_rope_tables(past + jnp.arange(dec, dtype=jnp.int32)), (tm_s // dec, 1))
    page_table = page_table.astype(jnp.int32)
    hp = x_prompt.reshape(np_, d)
    hs = x_sample.reshape(ns_, d)
    nsa_p_l, nsa_s_l, mla_p_l, mla_s_l, win_p_l, win_s_l = [], [], [], [], [], []
    for l in range(depth):
        w1, w_ga, w_gb, wq2, wuk = _arrange_proj_weights(w_in[l], mla_w_uq[l], mla_w_uk[l])
        pe, cw1, cw2 = _arrange_compress_weights(cmp_pe_k[l], cmp_pe_v[l], cmp_k_w1[l], cmp_k_w2[l],
                                                 cmp_v_w1[l], cmp_v_w2[l])
        wuv = _arrange_wuv(mla_w_uv[l])
        tw = _tail_weights((w_ga, w_gb), w_up_a[l], w_up_b[l], w_o[l], ln1_g[l], ln1_b[l], w_group[l], b_group[l],
                           w_router[l], b_router[l], exp_w_gate[l], exp_w_up[l], exp_w_down[l],
                           ln2_g[l], ln2_b[l], w_pe[l], w_pg[l])
        qn, kvn = mla_q_norm[l][None], mla_kv_norm[l][None]
        qw, nsa, cmp, selbf, win, winbf, gt, qmla, mla, kmla = _project(hp, tab_p, s // tm_p, w1, qn, kvn, wq2, wuk,
                                                                         tm=tm_p)
        r3 = lambda a: a.reshape(b, s, -1)
        kcvc = _compress_prompt(r3(cmp), pe, cw1, cw2)
        on = _nsa_prompt(r3(qw), r3(gt), kcvc, r3(selbf), r3(winbf)).reshape(np_, -1)
        om = _mla_prompt(r3(qmla), r3(kmla), wuv).reshape(np_, -1)
        hp = _layer_tail(hp, on, om, p_prompt[l].reshape(np_, -1), tw, alpha, tm=tm_p)
        nsa_p_l.append(nsa.reshape(b, s, 4, NSA_GROUPS, NSA_DH))
        mla_p_l.append(r3(mla))
        win_p_l.append(win.reshape(b, s, 2, NSA_GROUPS, NSA_DH)[:, -min(WINDOW, s):])
        qw, nsa, cmp, selbf, win, winbf, gt, qmla, mla, kmla = _project(hs, tab_s, 1, w1, qn, kvn, wq2, wuk, tm=tm_s)
        r3 = lambda a: a.reshape(db, dec, -1)
        pool_nsa = jnp.transpose(cache_nsa_kv[l], (0, 2, 3, 4, 1)).reshape(n_pool, 4 * KVW, page)
        pool_mla = jnp.transpose(cache_mla[l], (0, 2, 1))
        win_old = jnp.transpose(cache_nsa_win[l], (0, 2, 3, 4, 1)).reshape(db, 2 * KVW, wb)
        o_c, sel = _nsa_sample_cmp(page_table, pool_nsa, r3(qw), pe, cw1, cw2, past)
        on = _nsa_sample_sel(page_table, pool_nsa, r3(qw), r3(gt), o_c, sel, r3(nsa)[:, :, 2 * KVW:], win_old, r3(win),
                             past).reshape(ns_, -1)
        om = _mla_sample(page_table, pool_mla, r3(qmla), r3(mla), wuv).reshape(ns_, -1)
        hs = _layer_tail(hs, on, om, p_sample[l].reshape(ns_, -1), tw, alpha, tm=tm_s)
        nsa_s_l.append(nsa.reshape(db, dec, 4, NSA_GROUPS, NSA_DH))
        mla_s_l.append(r3(mla))
        win_new = win.reshape(db, dec, 2, NSA_GROUPS, NSA_DH)
        win_s_l.append(jnp.concatenate([cache_nsa_win[l], win_new], axis=1)[:, -wb:])
    return (hp.reshape(b, s, d), hs.reshape(db, dec, d), jnp.stack(nsa_p_l), jnp.stack(nsa_s_l),
            jnp.stack(mla_p_l), jnp.stack(mla_s_l), jnp.stack(win_p_l), jnp.stack(win_s_l))
```

```python
import functools

import numpy as np
import jax
import jax.numpy as jnp
from jax import lax
from jax.experimental import pallas as pl
from jax.experimental.pallas import tpu as pltpu

F32 = jnp.float32
BF = jnp.bfloat16

D_MODEL = 1024
NSA_HEADS = 8
NSA_GROUPS = 2
NSA_HPG = NSA_HEADS // NSA_GROUPS
NSA_DH = 64
CMP_BLOCK = 32
CMP_STRIDE = 16
CMP_HIDDEN = 64
SEL_BLOCK = 64
SEL_TOPK = 16
WINDOW = 512
FORCE_BONUS = 1000.0
MLA_HEADS = 8
MLA_Q_RANK = 256
MLA_KV_RANK = 128
MLA_NOPE = 64
MLA_ROPE = 32
MLA_V = 64
ROPE_THETA = 10000.0
N_GROUPS = 4
EXPERTS_PER_GROUP = 8
N_EXPERTS = N_GROUPS * EXPERTS_PER_GROUP
D_EXPERT = 256
MOE_BLOCK = 128
PLE_DIM = 256
Q_BLOCK = 128
LN_EPS = 1e-5
RMS_EPS = 1e-6
KVW = NSA_GROUPS * NSA_DH
LANES = 128
HEAD_LANES = 128
MLA_LANES = 256
NSA_SCALE = NSA_DH ** -0.5
MLA_SCALE = (MLA_NOPE + MLA_ROPE) ** -0.5
NEG_INF = float("-inf")

VMEM_LIMIT = 48 * 1024 * 1024


def _cparams(sem):
    return pltpu.CompilerParams(dimension_semantics=sem, vmem_limit_bytes=VMEM_LIMIT)


def _dot(a, b):
    return jnp.dot(a, b, preferred_element_type=F32)


def _dot_nt(a, b):
    return lax.dot_general(a, b, (((1,), (1,)), ((), ())), preferred_element_type=F32)


def _rms(x, g):
    return x * lax.rsqrt(jnp.mean(x * x, axis=-1, keepdims=True) + RMS_EPS) * g


def _ln(x, g, b):
    mu = jnp.mean(x, axis=-1, keepdims=True)
    var = jnp.mean(jnp.square(x - mu), axis=-1, keepdims=True)
    return (x - mu) * lax.rsqrt(var + LN_EPS) * g + b


_C_QW = 0
_C_NSA = _C_QW + NSA_HEADS * HEAD_LANES
_C_WIN = _C_NSA + 4 * KVW
_C_GT = _C_WIN + 2 * KVW
_C_DQ = _C_GT + LANES
_C_DKV = _C_DQ + MLA_Q_RANK
_C_KR = _C_DKV + MLA_KV_RANK
_C_KRS = _C_KR + LANES
_C_END = _C_KRS + LANES
_Q2_NOPE = MLA_HEADS * MLA_NOPE
_Q2_R1 = _Q2_NOPE
_Q2_R2 = _Q2_R1 + MLA_HEADS * LANES
_Q2_END = _Q2_R2 + MLA_HEADS * LANES


def _proj_kernel(x_ref, w1_ref, tab_ref, qn_ref, kvn_ref, wq2_ref, wuk_ref,
                 qw_ref, nsa_ref, kcmp_ref, vcmp_ref, selbf_ref, win_ref, winbf_ref, gates_ref, qmla_ref, mla_ref,
                 kmla_ref):
    z = _dot(x_ref[...].astype(BF), w1_ref[...])
    qw_ref[...] = (z[:, _C_QW:_C_NSA] * NSA_SCALE).astype(BF)
    nsa = z[:, _C_NSA:_C_WIN]
    nsa_ref[...] = nsa
    kcmp_ref[...] = nsa[:, 0:KVW]
    vcmp_ref[...] = nsa[:, KVW:2 * KVW]
    selbf_ref[...] = nsa[:, 2 * KVW:].astype(BF)
    win = z[:, _C_WIN:_C_GT]
    win_ref[...] = win
    winbf_ref[...] = win.astype(BF)
    gates_ref[...] = z[:, _C_GT:_C_DQ]
    rn = _rms(z[:, _C_DQ:_C_DKV], qn_ref[...])
    r = _dot(rn.astype(BF), wq2_ref[...])
    lat = _dot(r[:, :_Q2_NOPE].astype(BF), wuk_ref[...])
    tab = tab_ref[...]
    c1, c2 = tab[:, 0:LANES], tab[:, LANES:2 * LANES]
    ck, sk = tab[:, 2 * LANES:3 * LANES], tab[:, 3 * LANES:4 * LANES]
    for h in range(MLA_HEADS):
        base = h * MLA_LANES
        qmla_ref[:, base:base + LANES] = lat[:, h * LANES:(h + 1) * LANES].astype(BF)
        pe = (r[:, _Q2_R1 + h * LANES:_Q2_R1 + (h + 1) * LANES] * c1
              + r[:, _Q2_R2 + h * LANES:_Q2_R2 + (h + 1) * LANES] * c2)
        qmla_ref[:, base + LANES:base + MLA_LANES] = pe.astype(BF)
    ckv = _rms(z[:, _C_DKV:_C_KR], kvn_ref[...])
    kpe = z[:, _C_KR:_C_KRS] * ck + z[:, _C_KRS:_C_END] * sk
    mla_ref[:, 0:MLA_KV_RANK] = ckv
    mla_ref[:, MLA_KV_RANK:MLA_KV_RANK + MLA_ROPE] = kpe[:, 0:MLA_ROPE]
    kmla_ref[:, 0:LANES] = ckv.astype(BF)
    ones_lane = lax.broadcasted_iota(jnp.int32, kpe.shape, 1) == MLA_ROPE
    kmla_ref[:, LANES:MLA_LANES] = jnp.where(ones_lane, 1.0, kpe).astype(BF)


def _rope_tables(pos):
    half = MLA_ROPE // 2
    inv = ROPE_THETA ** (-jnp.arange(half, dtype=F32) / half)
    ang = pos.astype(F32)[:, None] * inv
    cos, sin = jnp.cos(ang), jnp.sin(ang)
    pad = jnp.zeros((pos.shape[0], LANES - MLA_ROPE), F32)
    return jnp.concatenate([cos, sin, pad, -sin, cos, pad, cos, cos, pad, -sin, sin, pad], axis=1)


def _arrange_proj_weights(w_in, w_uq, w_uk):
    d = w_in.shape[0]
    o = np.cumsum((0, NSA_HEADS * NSA_DH, KVW, KVW, KVW, KVW, KVW, KVW, 3 * NSA_HEADS,
                   MLA_Q_RANK, MLA_KV_RANK, MLA_ROPE, D_MODEL, D_MODEL))
    zeros = lambda n: jnp.zeros((d, n), w_in.dtype)
    cols = []
    for c in range(NSA_HEADS):
        wq = w_in[:, c * NSA_DH:(c + 1) * NSA_DH]
        cols += [wq, zeros(NSA_DH)] if c < NSA_HPG else [zeros(NSA_DH), wq]
    cols.append(w_in[:, o[1]:o[7]])
    cols += [w_in[:, o[7]:o[8]], zeros(LANES - 3 * NSA_HEADS)]
    cols.append(w_in[:, o[8]:o[10]])
    kr = w_in[:, o[10]:o[11]]
    half = MLA_ROPE // 2
    cols += [kr, zeros(LANES - MLA_ROPE), kr[:, half:], kr[:, :half], zeros(LANES - MLA_ROPE)]
    w1 = jnp.concatenate(cols, axis=1).astype(BF)
    w_ga = w_in[:, o[11]:o[12]].astype(BF)
    w_gb = w_in[:, o[12]:o[13]].astype(BF)

    uq = w_uq.reshape(MLA_Q_RANK, MLA_HEADS, MLA_NOPE + MLA_ROPE)
    zq = jnp.zeros((MLA_Q_RANK, LANES - MLA_ROPE), w_uq.dtype)
    q2 = [uq[:, :, :MLA_NOPE].reshape(MLA_Q_RANK, -1)]
    for lo in (MLA_NOPE, MLA_NOPE + half):
        for h in range(MLA_HEADS):
            x = uq[:, h, lo:lo + half]
            q2 += [x, x, zq]
    wq2 = jnp.concatenate(q2, axis=1).astype(BF)
    wuk = jnp.zeros((MLA_HEADS * MLA_NOPE, MLA_HEADS * LANES), w_uk.dtype)
    for h in range(MLA_HEADS):
        wuk = wuk.at[h * MLA_NOPE:(h + 1) * MLA_NOPE, h * LANES:(h + 1) * LANES].set(w_uk[:, h, :].T)
    return w1, w_ga, w_gb, wq2, wuk.astype(BF)


def _project(x2d, tab, tab_blocks, w1, qn, kvn, wq2, wuk, tm=256):
    n = x2d.shape[0]
    grid = (n // tm,)
    row = lambda w: pl.BlockSpec((tm, w), lambda i: (i, 0))
    full = lambda a: pl.BlockSpec(a.shape, lambda i: (0,) * a.ndim)
    outs = [(NSA_HEADS * HEAD_LANES, BF), (4 * KVW, F32), (KVW, F32), (KVW, F32), (2 * KVW, BF), (2 * KVW, F32),
            (2 * KVW, BF),
            (LANES, F32), (MLA_HEADS * MLA_LANES, BF), (MLA_KV_RANK + MLA_ROPE, F32), (MLA_LANES, BF)]
    return pl.pallas_call(
        _proj_kernel,
        grid=grid,
        in_specs=[row(D_MODEL), full(w1), pl.BlockSpec((tm, 4 * LANES), lambda i: (i % tab_blocks, 0)),
                  full(qn), full(kvn), full(wq2), full(wuk)],
        out_specs=[row(w) for w, _ in outs],
        out_shape=[jax.ShapeDtypeStruct((n, w), dt) for w, dt in outs],
        compiler_params=_cparams(("parallel",)),
        name="proj",
    )(x2d, w1, tab, qn, kvn, wq2, wuk)


def _compress_partial(chunk_rows, pe_ref, w1_ref):
    a = b = None
    for l in range(CMP_STRIDE):
        xl = chunk_rows(l)
        ta = _dot((xl + pe_ref[l:l + 1, :]).astype(BF), w1_ref[l])
        tb = _dot((xl + pe_ref[CMP_STRIDE + l:CMP_STRIDE + l + 1, :]).astype(BF), w1_ref[CMP_STRIDE + l])
        a = ta if a is None else a + ta
        b = tb if b is None else b + tb
    return a, b


def _compress_kernel(k_ref, v_ref, pe_ref, w1_ref, w2_ref, o_ref, b_scr):
    nchunk = o_ref.shape[1]
    a, b = _compress_partial(
        lambda l: jnp.concatenate([k_ref[0, pl.ds(l, nchunk, stride=CMP_STRIDE), :],
                                   v_ref[0, pl.ds(l, nchunk, stride=CMP_STRIDE), :]], axis=1), pe_ref, w1_ref)
    b_scr[0:nchunk, :] = b
    b_scr[nchunk:nchunk + 8, :] = jnp.zeros((8, 4 * NSA_DH), F32)
    pre = a + b_scr[pl.ds(1, nchunk), :]
    o_ref[0] = _dot(jax.nn.gelu(pre).astype(BF), w2_ref[...]).astype(BF)


def _arrange_compress_weights(pe_k, pe_v, k_w1, k_w2, v_w1, v_w2):
    dh, hid = NSA_DH, CMP_HIDDEN
    pe = jnp.concatenate([pe_k, pe_k, pe_v, pe_v], axis=1)
    w1k = k_w1.reshape(CMP_BLOCK, dh, hid)
    w1v = v_w1.reshape(CMP_BLOCK, dh, hid)
    w1 = jnp.zeros((CMP_BLOCK, 4 * dh, 4 * hid), k_w1.dtype)
    w2 = jnp.zeros((4 * hid, 4 * dh), k_w2.dtype)
    for blk, (a1, a2) in enumerate(((w1k, k_w2), (w1k, k_w2), (w1v, v_w2), (w1v, v_w2))):
        w1 = w1.at[:, blk * dh:(blk + 1) * dh, blk * hid:(blk + 1) * hid].set(a1)
        w2 = w2.at[blk * hid:(blk + 1) * hid, blk * dh:(blk + 1) * dh].set(a2)
    return pe, w1.astype(BF), w2.astype(BF)


def _compress_prompt(kcmp_rows, vcmp_rows, pe, w1, w2):
    b, s, _ = kcmp_rows.shape
    nchunk = s // CMP_STRIDE
    full = lambda a: pl.BlockSpec(a.shape, lambda i: (0,) * a.ndim)
    rows = pl.BlockSpec((1, s, KVW), lambda i: (i, 0, 0))
    return pl.pallas_call(
        _compress_kernel,
        grid=(b,),
        in_specs=[rows, rows, full(pe), full(w1), full(w2)],
        out_specs=pl.BlockSpec((1, nchunk, 4 * NSA_DH), lambda i: (i, 0, 0)),
        out_shape=jax.ShapeDtypeStruct((b, nchunk, 4 * NSA_DH), BF),
        scratch_shapes=[pltpu.VMEM((nchunk + 8, 4 * NSA_DH), F32)],
        compiler_params=_cparams(("parallel",)),
        name="compress_prompt",
    )(kcmp_rows, vcmp_rows, pe, w1, w2)


def _masked_softmax(s, mask):
    s = jnp.where(mask, s, NEG_INF)
    m = jnp.max(s, axis=-1, keepdims=True)
    m = jnp.where(m == NEG_INF, 0.0, m)
    e = jnp.exp(s - m)
    den = jnp.sum(e, axis=-1, keepdims=True)
    return e / jnp.where(den > 0, den, 1.0)


def _online_update(s, v, m_ref, l_ref, acc_ref, v_transposed=False):
    m_old = m_ref[...]
    m_new = jnp.maximum(m_old, jnp.max(s, axis=-1, keepdims=True))
    m_safe = jnp.where(m_new == NEG_INF, 0.0, m_new)
    alpha = jnp.exp(m_old - m_safe)
    e = jnp.exp(s - m_safe)
    l_ref[...] = alpha * l_ref[...] + jnp.sum(e, axis=-1, keepdims=True)
    pv = _dot_nt(e.astype(BF), v) if v_transposed else _dot(e.astype(BF), v)
    acc_ref[...] = alpha * acc_ref[...] + pv
    m_ref[...] = m_new


def _online_init(m_ref, l_ref, acc_ref):
    m_ref[...] = jnp.full(m_ref.shape, NEG_INF, F32)
    l_ref[...] = jnp.zeros(l_ref.shape, F32)
    acc_ref[...] = jnp.zeros(acc_ref.shape, F32)


def _online_result(l_ref, acc_ref):
    l = l_ref[...]
    return acc_ref[...] / jnp.where(l > 0, l, 1.0)


def _alibi_slope_rows(g, rows_per_head):
    hh = lax.broadcasted_iota(jnp.int32, (NSA_HPG * rows_per_head, 1), 0) // rows_per_head
    slope = jnp.zeros(hh.shape, F32)
    for h in range(NSA_HPG):
        slope = jnp.where(hh == h, 2.0 ** -(g * NSA_HPG + h + 1), slope)
    return slope


def _select_topk(score, n_lanes):
    j = lax.broadcasted_iota(jnp.int32, score.shape, 1)
    sel = jnp.zeros(score.shape, F32)
    for _ in range(SEL_TOPK):
        m = jnp.max(score, axis=-1, keepdims=True)
        idx = jnp.min(jnp.where(score == m, j, n_lanes), axis=-1, keepdims=True)
        pick = (j == idx) & (m > NEG_INF)
        sel = jnp.where(pick, 1.0, sel)
        score = jnp.where(pick, NEG_INF, score)
    return sel


def _split_dot(x, w):
    hi = x.astype(BF)
    lo = (x - hi.astype(F32)).astype(BF)
    return _dot(hi, w) + _dot(lo, w)


SEL_TILE = 2048


def _nsa_prompt_kernel(qw_ref, gt_ref, kcvc_ref, sel_ref, win_ref, ovl_ref, o_ref, *scratch, nc):
    tq = qw_ref.shape[1]
    ncp = kcvc_ref.shape[1]
    q0 = pl.program_id(1) * tq
    rows = NSA_HPG * tq
    m_refs, l_refs, acc_refs = scratch[0:2], scratch[2:4], scratch[4:6]
    qi = lax.broadcasted_iota(jnp.int32, (rows, 1), 0) % tq
    q_pos = q0 + qi
    qp1 = q0 + lax.broadcasted_iota(jnp.int32, (tq, 1), 0)
    kc = kcvc_ref[0, :, 0:KVW]
    vc = kcvc_ref[0, :, KVW:2 * KVW]
    gates = jax.nn.sigmoid(gt_ref[0])
    wstart = pl.multiple_of(jnp.maximum(q0 - WINDOW, 0), tq)
    wk = WINDOW + tq
    kw = win_ref[0, pl.ds(wstart, wk), 0:KVW]
    vw = win_ref[0, pl.ds(wstart, wk), KVW:2 * KVW]
    jblk = lax.broadcasted_iota(jnp.int32, (tq, LANES), 1)
    c = lax.broadcasted_iota(jnp.int32, (1, ncp), 1)
    c_end = c * CMP_STRIDE + CMP_BLOCK - 1
    rel_c = (c_end - q0).astype(F32)
    wpos = wstart + lax.broadcasted_iota(jnp.int32, (1, wk), 1)
    rel_w = (wpos - q0).astype(F32)
    dist_w = q_pos - wpos

    qs, slopes, o_cs, o_ws, sels = [], [], [], [], []
    for g in range(NSA_GROUPS):
        q = jnp.concatenate([qw_ref[0, :, (g * NSA_HPG + h) * HEAD_LANES:(g * NSA_HPG + h + 1) * HEAD_LANES]
                             for h in range(NSA_HPG)], axis=0)
        slope = _alibi_slope_rows(g, tq)
        qs.append(q)
        slopes.append(slope)
        p_c = _masked_softmax(_dot_nt(q, kc) + slope * rel_c, (c_end <= q_pos) & (c < nc))
        o_cs.append(_dot(p_c.astype(BF), vc))
        p_sum = p_c[0:tq]
        for h in range(1, NSA_HPG):
            p_sum = p_sum + p_c[h * tq:(h + 1) * tq]
        imp = _split_dot(p_sum, ovl_ref[...])
        cur = qp1 // SEL_BLOCK
        avail = jblk * SEL_BLOCK <= qp1
        forced = (jblk == 0) | (jblk == cur) | (jblk == cur - 1)
        score = jnp.where(avail, imp + FORCE_BONUS * forced.astype(F32), NEG_INF)
        sels.append(_select_topk(score, LANES).astype(BF))
        p_w = _masked_softmax(_dot_nt(q, kw) + slope * rel_w, (dist_w >= 0) & (dist_w < WINDOW))
        o_ws.append(_dot(p_w.astype(BF), vw))
        _online_init(m_refs[g], l_refs[g], acc_refs[g])

    def tile(t, causal):
        k0 = pl.multiple_of(t * SEL_TILE, SEL_TILE)
        kk = sel_ref[0, pl.ds(k0, SEL_TILE), 0:KVW]
        vv = sel_ref[0, pl.ds(k0, SEL_TILE), KVW:2 * KVW]
        kpos = k0 + lax.broadcasted_iota(jnp.int32, (1, SEL_TILE), 1)
        rel = (kpos - q0).astype(F32)
        expand = lax.broadcasted_iota(jnp.int32, (LANES, SEL_TILE), 0) == kpos // SEL_BLOCK
        expand = jnp.where(expand, 1.0, 0.0).astype(BF)
        for g in range(NSA_GROUPS):
            picked = _dot(sels[g], expand) > 0.5
            s = (_dot_nt(qs[g], kk) + slopes[g] * rel).reshape(NSA_HPG, tq, SEL_TILE)
            s = jnp.where(picked[None], s, NEG_INF).reshape(rows, SEL_TILE)
            if causal:
                s = jnp.where(kpos <= q_pos, s, NEG_INF)
            _online_update(s, vv, m_refs[g], l_refs[g], acc_refs[g])

    def step(t, carry):
        tile(t, False)
        return carry

    n_full = q0 // SEL_TILE
    lax.fori_loop(0, n_full, step, 0)
    tile(n_full, True)
    for g in range(NSA_GROUPS):
        o_s = _online_result(l_refs[g], acc_refs[g])
        for h in range(NSA_HPG):
            head = g * NSA_HPG + h
            r = slice(h * tq, (h + 1) * tq)
            out = (gates[:, 3 * head:3 * head + 1] * o_cs[g][r]
                   + gates[:, 3 * head + 1:3 * head + 2] * o_s[r]
                   + gates[:, 3 * head + 2:3 * head + 3] * o_ws[g][r])
            o_ref[0, :, head * HEAD_LANES:(head + 1) * HEAD_LANES] = out.astype(BF)


def _overlap_matrix(ncp, nc):
    c0 = np.arange(ncp)[:, None] * CMP_STRIDE
    c1 = c0 + CMP_BLOCK - 1
    s0 = np.arange(LANES)[None, :] * SEL_BLOCK
    s1 = s0 + SEL_BLOCK - 1
    ok = (c0 <= s1) & (c1 >= s0) & (np.arange(ncp)[:, None] < nc)
    return jnp.asarray(ok.astype(np.float32)).astype(BF)


def _nsa_prompt(qw, gates, kcvc, sel_bf, win_bf):
    b, s, _ = qw.shape
    tq = Q_BLOCK
    ncp = kcvc.shape[1]
    nc = (s - CMP_BLOCK) // CMP_STRIDE + 1
    ovl = _overlap_matrix(ncp, nc)
    rows = NSA_HPG * tq
    return pl.pallas_call(
        functools.partial(_nsa_prompt_kernel, nc=nc),
        grid=(b, s // tq),
        in_specs=[pl.BlockSpec((1, tq, NSA_HEADS * HEAD_LANES), lambda bi, i: (bi, i, 0)),
                  pl.BlockSpec((1, tq, LANES), lambda bi, i: (bi, i, 0)),
                  pl.BlockSpec((1, ncp, 2 * KVW), lambda bi, i: (bi, 0, 0)),
                  pl.BlockSpec((1, s, 2 * KVW), lambda bi, i: (bi, 0, 0)),
                  pl.BlockSpec((1, s, 2 * KVW), lambda bi, i: (bi, 0, 0)),
                  pl.BlockSpec(ovl.shape, lambda bi, i: (0, 0))],
        out_specs=pl.BlockSpec((1, tq, NSA_HEADS * HEAD_LANES), lambda bi, i: (bi, i, 0)),
        out_shape=jax.ShapeDtypeStruct((b, s, NSA_HEADS * HEAD_LANES), BF),
        scratch_shapes=([pltpu.VMEM((rows, 1), F32)] * (2 * NSA_GROUPS) + [pltpu.VMEM((rows, KVW), F32)] * NSA_GROUPS),
        compiler_params=_cparams(("parallel", "arbitrary")),
        name="nsa_prompt",
    )(qw, gates, kcvc, sel_bf, win_bf, ovl)


MLA_TILE = 2048


MLA_ONES_LANE = MLA_KV_RANK + MLA_ROPE


MLA_CHAINS = 4


def _mla_prompt_kernel(q_ref, k_ref, wuv_ref, o_ref, *scratch):
    tq = q_ref.shape[1]
    q0 = pl.program_id(1) * tq
    hpc = MLA_HEADS // MLA_CHAINS
    rows = hpc * tq
    m_refs, acc_refs = scratch[:MLA_CHAINS], scratch[MLA_CHAINS:]
    qs = [jnp.concatenate([q_ref[0, :, h * MLA_LANES:(h + 1) * MLA_LANES] for h in range(c * hpc, (c + 1) * hpc)],
                          axis=0) for c in range(MLA_CHAINS)]
    q_pos = q0 + lax.broadcasted_iota(jnp.int32, (rows, 1), 0) % tq
    for c in range(MLA_CHAINS):
        m_refs[c][...] = jnp.full((rows, 1), NEG_INF, F32)
        acc_refs[c][...] = jnp.zeros((rows, MLA_LANES), F32)

    def tile(t, causal):
        k0 = pl.multiple_of(t * MLA_TILE, MLA_TILE)
        kk = k_ref[0, pl.ds(k0, MLA_TILE), :]
        for c in range(MLA_CHAINS):
            s = _dot_nt(qs[c], kk) * MLA_SCALE
            if causal:
                kpos = k0 + lax.broadcasted_iota(jnp.int32, (1, MLA_TILE), 1)
                s = jnp.where(kpos <= q_pos, s, NEG_INF)
            m_old = m_refs[c][...]
            m_new = jnp.maximum(m_old, jnp.max(s, axis=-1, keepdims=True))
            m_refs[c][...] = m_new
            e = jnp.exp(s - m_new)
            acc_refs[c][...] = jnp.exp(m_old - m_new) * acc_refs[c][...] + _dot(e.astype(BF), kk)

    def step(t, carry):
        tile(t, False)
        return carry

    n_full = q0 // MLA_TILE
    lax.fori_loop(0, n_full, step, 0)
    tile(n_full, True)
    out = None
    for c in range(MLA_CHAINS):
        acc = acc_refs[c][...]
        o_lat = (acc[:, 0:MLA_KV_RANK] / acc[:, MLA_ONES_LANE:MLA_ONES_LANE + 1]).astype(BF)
        for i in range(hpc):
            part = _dot(o_lat[i * tq:(i + 1) * tq], wuv_ref[c * hpc + i])
            out = part if out is None else out + part
    o_ref[0] = out.astype(BF)


def _arrange_wuv(w_uv):
    w = jnp.zeros((MLA_HEADS, MLA_KV_RANK, MLA_HEADS * MLA_V), w_uv.dtype)
    for h in range(MLA_HEADS):
        w = w.at[h, :, h * MLA_V:(h + 1) * MLA_V].set(w_uv[:, h, :])
    return w.astype(BF)


def _mla_prompt(qmla, kmla, wuv):
    b, s, _ = qmla.shape
    tq = Q_BLOCK
    rows = MLA_HEADS * tq
    return pl.pallas_call(
        _mla_prompt_kernel,
        grid=(b, s // tq),
        in_specs=[pl.BlockSpec((1, tq, MLA_HEADS * MLA_LANES), lambda bi, i: (bi, i, 0)),
                  pl.BlockSpec((1, s, MLA_LANES), lambda bi, i: (bi, 0, 0)),
                  pl.BlockSpec(wuv.shape, lambda bi, i: (0, 0, 0))],
        out_specs=pl.BlockSpec((1, tq, MLA_HEADS * MLA_V), lambda bi, i: (bi, i, 0)),
        out_shape=jax.ShapeDtypeStruct((b, s, MLA_HEADS * MLA_V), BF),
        scratch_shapes=([pltpu.VMEM((rows // MLA_CHAINS, 1), F32)] * MLA_CHAINS
                        + [pltpu.VMEM((rows // MLA_CHAINS, MLA_LANES), F32)] * MLA_CHAINS),
        compiler_params=_cparams(("parallel", "arbitrary")),
        name="mla_prompt",
    )(qmla, kmla, wuv)


_R_EXPERT0 = 32


def _merge_kernel(x_ref, on_ref, om_ref, wga_ref, wgb_ref, wa_ref, wb_ref, wo_ref, g1_ref, b1_ref,
                  wr_ref, br_ref, h_ref, route_ref, *, alpha):
    x = x_ref[...]
    xb = x.astype(BF)
    ga = jax.nn.sigmoid(_dot(xb, wga_ref[...]))
    gb = jax.nn.sigmoid(_dot(xb, wgb_ref[...]))
    mix = ga * _dot(on_ref[...], wa_ref[...]) + gb * _dot(om_ref[...], wb_ref[...])
    h = _ln(alpha * x + _dot(mix.astype(BF), wo_ref[...]), g1_ref[...], b1_ref[...])
    h_ref[...] = h
    logit = _dot(h.astype(BF), wr_ref[...]) + br_ref[...]
    lane = lax.broadcasted_iota(jnp.int32, logit.shape, 1)
    is_g = lane < N_GROUPS
    gl = jnp.where(is_g, logit, NEG_INF)
    gmax = jnp.max(gl, axis=-1, keepdims=True)
    gsel = jnp.min(jnp.where(gl == gmax, lane, LANES), axis=-1, keepdims=True)
    g_w = 1.0 / jnp.sum(jnp.exp(gl - gmax), axis=-1, keepdims=True)
    e_lane = lane - _R_EXPERT0
    in_grp = (e_lane >= 0) & (e_lane < N_EXPERTS) & (e_lane // EXPERTS_PER_GROUP == gsel)
    el = jnp.where(in_grp, logit, NEG_INF)
    v1 = jnp.max(el, axis=-1, keepdims=True)
    i1 = jnp.min(jnp.where(el == v1, e_lane, LANES), axis=-1, keepdims=True)
    el2 = jnp.where(e_lane == i1, NEG_INF, el)
    v2 = jnp.max(el2, axis=-1, keepdims=True)
    i2 = jnp.min(jnp.where(el2 == v2, e_lane, LANES), axis=-1, keepdims=True)
    e2 = jnp.exp(v2 - v1)
    den = 1.0 + e2
    route = jnp.where(lane == 0, i1.astype(F32),
                      jnp.where(lane == 1, i2.astype(F32),
                                jnp.where(lane == 2, g_w * (1.0 / den),
                                          jnp.where(lane == 3, g_w * (e2 / den), 0.0))))
    route_ref[...] = route


def _arrange_merge_weights(w_up_a, w_group, b_group, w_router, b_router):
    wa = jnp.zeros((NSA_HEADS * HEAD_LANES, D_MODEL), w_up_a.dtype)
    for c in range(NSA_HEADS):
        off = c * HEAD_LANES + (0 if c < NSA_HPG else NSA_DH)
        wa = wa.at[off:off + NSA_DH].set(w_up_a[c * NSA_DH:(c + 1) * NSA_DH])
    d = w_group.shape[0]
    wr = jnp.concatenate([w_group, jnp.zeros((d, _R_EXPERT0 - N_GROUPS), w_group.dtype), w_router,
                          jnp.zeros((d, LANES - _R_EXPERT0 - N_EXPERTS), w_group.dtype)], axis=1)
    br = jnp.concatenate([b_group, jnp.zeros((_R_EXPERT0 - N_GROUPS,), b_group.dtype), b_router,
                          jnp.zeros((LANES - _R_EXPERT0 - N_EXPERTS,), b_group.dtype)])[None]
    return wa.astype(BF), wr.astype(BF), br


def _merge(x2d, on, om, w_ga, w_gb, wa, wb, wo, g1, b1, wr, br, alpha, tm=256):
    n = x2d.shape[0]
    row = lambda w: pl.BlockSpec((tm, w), lambda i: (i, 0))
    full = lambda a: pl.BlockSpec(a.shape, lambda i: (0,) * a.ndim)
    consts = (w_ga, w_gb, wa, wb, wo, g1, b1, wr, br)
    return pl.pallas_call(
        functools.partial(_merge_kernel, alpha=alpha),
        grid=(n // tm,),
        in_specs=[row(D_MODEL), row(on.shape[1]), row(om.shape[1])] + [full(a) for a in consts],
        out_specs=[row(D_MODEL), row(LANES)],
        out_shape=[jax.ShapeDtypeStruct((n, D_MODEL), F32), jax.ShapeDtypeStruct((n, LANES), F32)],
        compiler_params=_cparams(("parallel",)),
        name="merge_ln1_route",
    )(x2d, on, om, *consts)


def _expert_kernel(blk_e_ref, src_ref, dst_ref, h_hbm, wg_ref, wu_ref, wd_ref, y_hbm, xbuf, ybuf, gsem, ssem):
    i = pl.program_id(0)
    nb = pl.num_programs(0)
    slot = i % 2
    d = xbuf.shape[2]

    def gather(blk, sl, r):
        return pltpu.make_async_copy(h_hbm.at[pl.ds(src_ref[blk * MOE_BLOCK + r], 1)], xbuf.at[sl, pl.ds(r, 1)],
                                     gsem.at[sl])

    def scatter(blk, sl, r):
        dst = dst_ref[blk * MOE_BLOCK + r]
        col = pl.multiple_of((dst & 1) * d, d)
        return pltpu.make_async_copy(ybuf.at[sl, pl.ds(r, 1)],
                                     y_hbm.at[pl.ds(lax.shift_right_logical(dst, 1), 1), pl.ds(col, d)], ssem.at[sl])

    def for_rows(fn):
        def body(r, c):
            fn(r)
            return c
        lax.fori_loop(0, MOE_BLOCK, body, 0, unroll=8)

    @pl.when(i == 0)
    def _():
        for_rows(lambda r: gather(i, slot, r).start())

    @pl.when(i + 1 < nb)
    def _():
        for_rows(lambda r: gather(i + 1, 1 - slot, r).start())

    for_rows(lambda r: gather(i, slot, r).wait())

    @pl.when(i >= 2)
    def _():
        for_rows(lambda r: scatter(i - 2, slot, r).wait())

    xb = xbuf[slot].astype(BF)
    hid = jax.nn.silu(_dot(xb, wg_ref[0])) * _dot(xb, wu_ref[0])
    ybuf[slot] = _dot(hid.astype(BF), wd_ref[0])
    for_rows(lambda r: scatter(i, slot, r).start())

    @pl.when(i == nb - 1)
    def _():
        for_rows(lambda r: scatter(i - 1, 1 - slot, r).wait())
        for_rows(lambda r: scatter(i, slot, r).wait())


def _experts(h, blk_e, src, dst, wg, wu, wd):
    n, d = h.shape
    nb = blk_e.shape[0]
    assert nb >= 2
    wspec = lambda a: pl.BlockSpec((1,) + a.shape[1:], lambda i, e, s, t: (e[i], 0, 0))
    return pl.pallas_call(
        _expert_kernel,
        grid_spec=pltpu.PrefetchScalarGridSpec(
            num_scalar_prefetch=3,
            grid=(nb,),
            in_specs=[pl.BlockSpec(memory_space=pl.ANY), wspec(wg), wspec(wu), wspec(wd)],
            out_specs=pl.BlockSpec(memory_space=pl.ANY),
            scratch_shapes=[pltpu.VMEM((2, MOE_BLOCK, d), F32), pltpu.VMEM((2, MOE_BLOCK, d), F32),
                            pltpu.SemaphoreType.DMA((2,)), pltpu.SemaphoreType.DMA((2,))]),
        out_shape=jax.ShapeDtypeStruct((n + MOE_BLOCK, 2 * d), F32),
        compiler_params=_cparams(("arbitrary",)),
        name="experts",
    )(blk_e, src, dst, h, wg, wu, wd)


def _route_plan(route, n):
    eid = route[:, 0:2].astype(jnp.int32).reshape(-1)
    m = eid.shape[0]
    order = jnp.argsort(eid)
    e_sorted = eid[order]
    counts = jnp.bincount(eid, length=N_EXPERTS)
    padded = (counts + MOE_BLOCK - 1) // MOE_BLOCK * MOE_BLOCK
    pad_end = jnp.cumsum(padded)
    start = jnp.cumsum(counts) - counts
    pad_start = pad_end - padded
    nb = -(-m // MOE_BLOCK) + N_EXPERTS
    blk_e = jnp.minimum(jnp.sum(pad_end[None, :] <= (jnp.arange(nb) * MOE_BLOCK)[:, None], axis=1),
                        N_EXPERTS - 1).astype(jnp.int32)
    s_all = jnp.arange(nb * MOE_BLOCK, dtype=jnp.int32)
    e_slot = jnp.repeat(blk_e, MOE_BLOCK)
    within = s_all - pad_start[e_slot].astype(jnp.int32)
    valid = within < counts[e_slot]
    assign = order[jnp.clip(start[e_slot] + within, 0, m - 1)].astype(jnp.int32)
    dump = m + MOE_BLOCK * ((s_all // MOE_BLOCK) % 2) + s_all % MOE_BLOCK
    src = jnp.where(valid, assign // 2, 0)
    dst = jnp.where(valid, assign, dump)
    return blk_e, src, dst


def _final_kernel(h_ref, y_ref, route_ref, g2_ref, b2_ref, wpg_ref, p_ref, wpe_ref, o_ref, *, alpha):
    h = h_ref[...]
    route = route_ref[...]
    moe = route[:, 2:3] * y_ref[:, 0:D_MODEL] + route[:, 3:4] * y_ref[:, D_MODEL:2 * D_MODEL]
    h2 = _ln(alpha * h + moe, g2_ref[...], b2_ref[...])
    gate = jax.nn.sigmoid(_dot(h2.astype(BF), wpg_ref[...]))
    o_ref[...] = h2 + gate * _dot(p_ref[...].astype(BF), wpe_ref[...])


def _final(h, y2, route, g2, b2, wpg, p2d, wpe, alpha, tm=256):
    n = h.shape[0]
    row = lambda w: pl.BlockSpec((tm, w), lambda i: (i, 0))
    full = lambda a: pl.BlockSpec(a.shape, lambda i: (0,) * a.ndim)
    return pl.pallas_call(
        functools.partial(_final_kernel, alpha=alpha),
        grid=(n // tm,),
        in_specs=[row(D_MODEL), row(2 * D_MODEL), row(LANES), full(g2), full(b2), full(wpg), row(PLE_DIM), full(wpe)],
        out_specs=row(D_MODEL),
        out_shape=jax.ShapeDtypeStruct((n, D_MODEL), F32),
        compiler_params=_cparams(("parallel",)),
        name="combine_ln2_ple",
    )(h, y2, route, g2, b2, wpg, p2d, wpe)


def _layer_tail(x2d, on, om, p2d, tw, alpha, tm=256):
    n = x2d.shape[0]
    h, route = _merge(x2d, on, om, tw["w_ga"], tw["w_gb"], tw["wa"], tw["wb"], tw["wo"], tw["g1"], tw["b1"],
                      tw["wr"], tw["br"], alpha, tm=tm)
    blk_e, src, dst = _route_plan(route, n)
    y = _experts(h, blk_e, src, dst, tw["wg"], tw["wu"], tw["wd"])
    return _final(h, y, route, tw["g2"], tw["b2"], tw["wpg"], p2d, tw["wpe"], alpha, tm=tm)


SAMPLE_PAGES_PER_STEP = 64
CMP_PAGES_PER_STEP = 64


def _stack_heads(ref, n_heads, lanes):
    return jnp.concatenate([ref[0, :, c * lanes:(c + 1) * lanes] for c in range(n_heads)], axis=0)


def _alibi_rows(first_head, n_heads, rows_per_head):
    hh = lax.broadcasted_iota(jnp.int32, (n_heads * rows_per_head, 1), 0) // rows_per_head
    slope = jnp.zeros(hh.shape, F32)
    for h in range(n_heads):
        slope = jnp.where(hh == h, 2.0 ** -(first_head + h + 1), slope)
    return slope


def _pad_rows(x, n):
    return jnp.concatenate([x, jnp.zeros((n - x.shape[0], x.shape[1]), x.dtype)], axis=0)


def _nsa_sample_cmp_kernel(pt_ref, qw_ref, pe_ref, w1_ref, w2_ref, ovl_ref, *rest, pps, past, nc):
    pages = rest[:pps]
    oc_ref, sel_ref, a_scr, b_scr, xk_scr, xv_scr = rest[pps:]
    step = pl.program_id(1)
    page = pages[0].shape[2]
    nstep = pps * page // CMP_STRIDE
    for k, pg in enumerate(pages):
        xk_scr[k * page:(k + 1) * page, :] = pg[0, 0:KVW, :].T
        xv_scr[k * page:(k + 1) * page, :] = pg[0, KVW:2 * KVW, :].T
    a, b = _compress_partial(
        lambda l: jnp.concatenate([xk_scr[pl.ds(l, nstep, stride=CMP_STRIDE), :],
                                   xv_scr[pl.ds(l, nstep, stride=CMP_STRIDE), :]], axis=1), pe_ref, w1_ref)
    r0 = pl.multiple_of(step * nstep, nstep)
    a_scr[pl.ds(r0, nstep), :] = a
    b_scr[pl.ds(r0, nstep), :] = b

    @pl.when(step == pl.num_programs(1) - 1)
    def _():
        _nsa_sample_select(qw_ref, w2_ref, ovl_ref, oc_ref, sel_ref, a_scr, b_scr, past=past, nc=nc)


def _nsa_sample_select(qw_ref, w2_ref, ovl_ref, oc_ref, sel_ref, a_scr, b_scr, *, past, nc):
    dec = qw_ref.shape[1]
    nchunk = a_scr.shape[0]
    nsl = sel_ref.shape[2]
    b_scr[nchunk:nchunk + 8, :] = jnp.zeros((8, 4 * NSA_DH), F32)
    pre = a_scr[...] + b_scr[pl.ds(1, nchunk), :]
    kcvc = _dot(jax.nn.gelu(pre).astype(BF), w2_ref[...]).astype(BF)
    kc = kcvc[:, 0:KVW]
    vc = kcvc[:, KVW:2 * KVW]
    rows = NSA_HPG * dec
    q_pos = past + lax.broadcasted_iota(jnp.int32, (rows, 1), 0) % dec
    qp1 = past + lax.broadcasted_iota(jnp.int32, (dec, 1), 0)
    c = lax.broadcasted_iota(jnp.int32, (1, nchunk), 1)
    jblk = lax.broadcasted_iota(jnp.int32, (dec, nsl), 1)
    for g in range(NSA_GROUPS):
        q = jnp.concatenate([qw_ref[0, :, (g * NSA_HPG + h) * HEAD_LANES:(g * NSA_HPG + h + 1) * HEAD_LANES]
                             for h in range(NSA_HPG)], axis=0)
        slope = _alibi_rows(g * NSA_HPG, NSA_HPG, dec)
        dist_c = q_pos - (c * CMP_STRIDE + CMP_BLOCK - 1)
        s_c = _dot_nt(q, kc) - slope * dist_c.astype(F32)
        p_c = _masked_softmax(s_c, (dist_c >= 0) & (c < nc))
        oc_ref[0, g * rows:(g + 1) * rows, :] = _dot(p_c.astype(BF), vc)
        p_sum = p_c[0:dec]
        for h in range(1, NSA_HPG):
            p_sum = p_sum + p_c[h * dec:(h + 1) * dec]
        imp = _split_dot(p_sum, ovl_ref[...])
        cur = qp1 // SEL_BLOCK
        avail = jblk * SEL_BLOCK <= qp1
        forced = (jblk == 0) | (jblk == cur) | (jblk == cur - 1)
        score = jnp.where(avail, imp + FORCE_BONUS * forced.astype(F32), NEG_INF)
        sel = _select_topk(score, nsl)
        sel_ref[0, g * rows:(g + 1) * rows, :] = jnp.concatenate([sel] * NSA_HPG, axis=0)


def _overlap_matrix_wide(ncp, nc, nsl):
    c0 = np.arange(ncp)[:, None] * CMP_STRIDE
    c1 = c0 + CMP_BLOCK - 1
    s0 = np.arange(nsl)[None, :] * SEL_BLOCK
    s1 = s0 + SEL_BLOCK - 1
    ok = (c0 <= s1) & (c1 >= s0) & (np.arange(ncp)[:, None] < nc)
    return jnp.asarray(ok.astype(np.float32)).astype(BF)


def _nsa_sample_cmp(page_table, pool, qw, pe, w1, w2, past):
    db, n_pages = page_table.shape
    page = pool.shape[2]
    dec = qw.shape[1]
    nchunk = n_pages * page // CMP_STRIDE
    nc = (past + dec - CMP_BLOCK) // CMP_STRIDE + 1
    ns = -(-(past + dec) // SEL_BLOCK)
    nsl = -(-ns // LANES) * LANES
    ovl = _overlap_matrix_wide(nchunk, nc, nsl)
    rows = NSA_HEADS * dec
    pps = min(CMP_PAGES_PER_STEP, n_pages)
    assert n_pages % pps == 0
    full = lambda a: pl.BlockSpec(a.shape, lambda s, j, pt: (0,) * a.ndim)
    page_spec = lambda k: pl.BlockSpec((1, 2 * KVW, page), lambda s, j, pt: (pt[s, j * pps + k], 0, 0))
    return pl.pallas_call(
        functools.partial(_nsa_sample_cmp_kernel, pps=pps, past=past, nc=nc),
        grid_spec=pltpu.PrefetchScalarGridSpec(
            num_scalar_prefetch=1,
            grid=(db, n_pages // pps),
            in_specs=[pl.BlockSpec((1, dec, NSA_HEADS * HEAD_LANES), lambda s, j, pt: (s, 0, 0)),
                      full(pe), full(w1), full(w2), full(ovl)] + [page_spec(k) for k in range(pps)],
            out_specs=[pl.BlockSpec((1, rows, KVW), lambda s, j, pt: (s, 0, 0)),
                       pl.BlockSpec((1, rows, nsl), lambda s, j, pt: (s, 0, 0))],
            scratch_shapes=[pltpu.VMEM((nchunk, 4 * NSA_DH), F32), pltpu.VMEM((nchunk + 8, 4 * NSA_DH), F32),
                            pltpu.VMEM((pps * page, KVW), F32), pltpu.VMEM((pps * page, KVW), F32)]),
        out_shape=[jax.ShapeDtypeStruct((db, rows, KVW), F32), jax.ShapeDtypeStruct((db, rows, nsl), F32)],
        compiler_params=pltpu.CompilerParams(dimension_semantics=("parallel", "arbitrary"),
                                             vmem_limit_bytes=56 * 1024 * 1024),
        name="nsa_sample_compress_select",
    )(page_table, qw, pe, w1, w2, ovl, *([pool] * pps))


def _nsa_sample_sel_kernel(pt_ref, qw_ref, gt_ref, oc_ref, selt_ref, self_ref, new_ref, winold_ref, winnew_ref,
                           *rest, pps, past):
    pages = rest[:pps]
    o_ref, m_ref, l_ref, acc_ref = rest[pps:]
    step = pl.program_id(1)
    dec = qw_ref.shape[1]
    page = pages[0].shape[2]
    rows = NSA_HEADS * dec
    nk = pps * page

    @pl.when(step == 0)
    def _():
        _online_init(m_ref, l_ref, acc_ref)

    q = _stack_heads(qw_ref, NSA_HEADS, HEAD_LANES)
    slope = _alibi_rows(0, NSA_HEADS, dec)
    q_pos = past + lax.broadcasted_iota(jnp.int32, (rows, 1), 0) % dec
    kt = jnp.concatenate([pg[0, 0:KVW, :] for pg in pages], axis=1).astype(BF)
    vt = jnp.concatenate([pg[0, KVW:2 * KVW, :] for pg in pages], axis=1).astype(BF)
    key = lax.broadcasted_iota(jnp.int32, (1, nk), 1)
    kpos = step * nk + key
    blocks_per_step = nk // SEL_BLOCK
    tile_off = (step % (LANES // blocks_per_step)) * blocks_per_step
    expand = lax.broadcasted_iota(jnp.int32, (LANES, nk), 0) == tile_off + key // SEL_BLOCK
    picked = _dot(selt_ref[0].astype(BF), jnp.where(expand, 1.0, 0.0).astype(BF))
    dist = q_pos - kpos
    s = _dot(q, kt) - slope * dist.astype(F32)
    _online_update(jnp.where((picked > 0.5) & (dist >= 0), s, NEG_INF), vt, m_ref, l_ref, acc_ref,
                   v_transposed=True)

    @pl.when(step == pl.num_programs(1) - 1)
    def _():
        inew = lax.broadcasted_iota(jnp.int32, (1, LANES), 1)
        dist_n = q_pos - (past + inew)
        valid_n = (inew < dec) & (dist_n >= 0)
        new = _pad_rows(new_ref[0], LANES)
        blk_new = past // SEL_BLOCK
        picked_n = self_ref[0][:, blk_new:blk_new + 1]
        s_n = _dot_nt(q, new[:, 0:KVW].astype(BF)) - slope * dist_n.astype(F32)
        _online_update(jnp.where(valid_n & (picked_n > 0.5), s_n, NEG_INF), new[:, KVW:2 * KVW].astype(BF),
                       m_ref, l_ref, acc_ref)
        o_s = _online_result(l_ref, acc_ref)
        wb = winold_ref.shape[2]
        wpos = past - wb + lax.broadcasted_iota(jnp.int32, (1, wb), 1)
        dist_o = q_pos - wpos
        s_o = _dot(q, winold_ref[0, 0:KVW, :].astype(BF)) - slope * dist_o.astype(F32)
        s_o = jnp.where((dist_o >= 0) & (dist_o < WINDOW) & (wpos >= 0), s_o, NEG_INF)
        wnew = _pad_rows(winnew_ref[0], LANES)
        s_w = _dot_nt(q, wnew[:, 0:KVW].astype(BF)) - slope * dist_n.astype(F32)
        s_w = jnp.where(valid_n & (dist_n < WINDOW), s_w, NEG_INF)
        mw = jnp.maximum(jnp.max(s_o, axis=-1, keepdims=True), jnp.max(s_w, axis=-1, keepdims=True))
        mw = jnp.where(mw == NEG_INF, 0.0, mw)
        e_o = jnp.exp(s_o - mw)
        e_w = jnp.exp(s_w - mw)
        den = jnp.sum(e_o, axis=-1, keepdims=True) + jnp.sum(e_w, axis=-1, keepdims=True)
        o_w = (_dot_nt(e_o.astype(BF), winold_ref[0, KVW:2 * KVW, :].astype(BF))
               + _dot(e_w.astype(BF), wnew[:, KVW:2 * KVW].astype(BF))) / jnp.where(den > 0, den, 1.0)
        gates = jax.nn.sigmoid(gt_ref[0])
        o_c = oc_ref[0]
        for head in range(NSA_HEADS):
            r = slice(head * dec, (head + 1) * dec)
            out = (gates[:, 3 * head:3 * head + 1] * o_c[r] + gates[:, 3 * head + 1:3 * head + 2] * o_s[r]
                   + gates[:, 3 * head + 2:3 * head + 3] * o_w[r])
            o_ref[0, :, head * HEAD_LANES:(head + 1) * HEAD_LANES] = out.astype(BF)


def _nsa_sample_sel(page_table, pool, qw, gates, o_c, sel, new_rows, win_old, win_new, past):
    db, n_pages = page_table.shape
    page = pool.shape[2]
    dec = qw.shape[1]
    pps = min(SAMPLE_PAGES_PER_STEP, n_pages)
    assert n_pages % pps == 0
    rows = NSA_HEADS * dec
    nsl = sel.shape[2]
    steps_per_tile = LANES // (pps * page // SEL_BLOCK)
    blk3 = lambda a: pl.BlockSpec((1,) + a.shape[1:], lambda s, j, pt: (s, 0, 0))
    page_spec = lambda k: pl.BlockSpec((1, 2 * KVW, page), lambda s, j, pt: (pt[s, j * pps + k], 1, 0))
    return pl.pallas_call(
        functools.partial(_nsa_sample_sel_kernel, pps=pps, past=past),
        grid_spec=pltpu.PrefetchScalarGridSpec(
            num_scalar_prefetch=1,
            grid=(db, n_pages // pps),
            in_specs=[blk3(qw), blk3(gates), blk3(o_c),
                      pl.BlockSpec((1, rows, LANES), lambda s, j, pt: (s, 0, j // steps_per_tile)),
                      blk3(sel), blk3(new_rows), blk3(win_old), blk3(win_new)] + [page_spec(k) for k in range(pps)],
            out_specs=pl.BlockSpec((1, dec, NSA_HEADS * HEAD_LANES), lambda s, j, pt: (s, 0, 0)),
            scratch_shapes=[pltpu.VMEM((rows, 1), F32), pltpu.VMEM((rows, 1), F32), pltpu.VMEM((rows, KVW), F32)]),
        out_shape=jax.ShapeDtypeStruct((db, dec, NSA_HEADS * HEAD_LANES), BF),
        compiler_params=_cparams(("parallel", "arbitrary")),
        name="nsa_sample_select_window",
    )(page_table, qw, gates, o_c, sel, sel, new_rows, win_old, win_new, *([pool] * pps))


def _mla_sample_kernel(pt_ref, q_ref, new_ref, wuv_ref, *rest, pps):
    pages = rest[:pps]
    o_ref, m_ref, l_ref, acc_ref = rest[pps:]
    step = pl.program_id(1)
    dec = q_ref.shape[1]
    kvw = MLA_KV_RANK + MLA_ROPE

    @pl.when(step == 0)
    def _():
        _online_init(m_ref, l_ref, acc_ref)

    q = _stack_heads(q_ref, MLA_HEADS, MLA_LANES)[:, 0:kvw]
    kt = jnp.concatenate([pg[0] for pg in pages], axis=1).astype(BF)
    _online_update(_dot(q, kt) * MLA_SCALE, kt, m_ref, l_ref, acc_ref, v_transposed=True)

    @pl.when(step == pl.num_programs(1) - 1)
    def _():
        rows = MLA_HEADS * dec
        qi = lax.broadcasted_iota(jnp.int32, (rows, 1), 0) % dec
        inew = lax.broadcasted_iota(jnp.int32, (1, LANES), 1)
        new = _pad_rows(new_ref[0], LANES).astype(BF)
        s_n = jnp.where((inew < dec) & (inew <= qi), _dot_nt(q, new) * MLA_SCALE, NEG_INF)
        _online_update(s_n, new, m_ref, l_ref, acc_ref)
        o_lat = _online_result(l_ref, acc_ref)[:, 0:MLA_KV_RANK].astype(BF)
        out = _dot(o_lat[0:dec], wuv_ref[0])
        for h in range(1, MLA_HEADS):
            out = out + _dot(o_lat[h * dec:(h + 1) * dec], wuv_ref[h])
        o_ref[0] = out.astype(BF)


def _mla_sample(page_table, pool, qmla, new_rows, wuv):
    db, n_pages = page_table.shape
    kvw, page = pool.shape[1:]
    dec = qmla.shape[1]
    pps = min(SAMPLE_PAGES_PER_STEP, n_pages)
    assert n_pages % pps == 0
    rows = MLA_HEADS * dec
    blk3 = lambda a: pl.BlockSpec((1,) + a.shape[1:], lambda s, j, pt: (s, 0, 0))
    page_spec = lambda k: pl.BlockSpec((1, kvw, page), lambda s, j, pt: (pt[s, j * pps + k], 0, 0))
    return pl.pallas_call(
        functools.partial(_mla_sample_kernel, pps=pps),
        grid_spec=pltpu.PrefetchScalarGridSpec(
            num_scalar_prefetch=1,
            grid=(db, n_pages // pps),
            in_specs=[blk3(qmla), blk3(new_rows), pl.BlockSpec(wuv.shape, lambda s, j, pt: (0, 0, 0))]
                     + [page_spec(k) for k in range(pps)],
            out_specs=pl.BlockSpec((1, dec, MLA_HEADS * MLA_V), lambda s, j, pt: (s, 0, 0)),
            scratch_shapes=[pltpu.VMEM((rows, 1), F32), pltpu.VMEM((rows, 1), F32), pltpu.VMEM((rows, kvw), F32)]),
        out_shape=jax.ShapeDtypeStruct((db, dec, MLA_HEADS * MLA_V), BF),
        compiler_params=_cparams(("parallel", "arbitrary")),
        name="mla_sample",
    )(page_table, qmla, new_rows, wuv, *([pool] * pps))


def _tail_weights(w_in_parts, w_up_a, w_up_b, w_o, ln1_g, ln1_b, w_group, b_group, w_router, b_router,
                  exp_w_gate, exp_w_up, exp_w_down, ln2_g, ln2_b, w_pe, w_pg):
    w_ga, w_gb = w_in_parts
    wa, wr, br = _arrange_merge_weights(w_up_a, w_group, b_group, w_router, b_router)
    return dict(w_ga=w_ga, w_gb=w_gb, wa=wa, wb=w_up_b.astype(BF), wo=w_o.astype(BF), g1=ln1_g[None], b1=ln1_b[None],
                wr=wr, br=br, wg=exp_w_gate.astype(BF), wu=exp_w_up.astype(BF), wd=exp_w_down.astype(BF),
                g2=ln2_g[None], b2=ln2_b[None], wpg=w_pg.astype(BF), wpe=w_pe.astype(BF))


def kernel(x_prompt, x_sample, cache_nsa_kv, cache_mla, cache_nsa_win, page_table, p_prompt, p_sample,
           w_in, cmp_pe_k, cmp_pe_v, cmp_k_w1, cmp_k_w2, cmp_v_w1, cmp_v_w2,
           mla_q_norm, mla_kv_norm, mla_w_uq, mla_w_uk, mla_w_uv,
           w_up_a, w_up_b, w_o, ln1_g, ln1_b,
           w_group, b_group, w_router, b_router, exp_w_gate, exp_w_up, exp_w_down,
           ln2_g, ln2_b, w_pe, w_pg):
    depth = w_in.shape[0]
    b, s, d = x_prompt.shape
    db, dec, _ = x_sample.shape
    n_pool, page = cache_nsa_kv.shape[1:3]
    past = page_table.shape[1] * page
    wb = cache_nsa_win.shape[2]
    alpha = (2.0 * depth) ** 0.25
    np_, ns_ = b * s, db * dec
    tm_p, tm_s = min(256, np_), min(256, ns_)
    tab_p = _rope_tables(jnp.arange(s, dtype=jnp.int32))
    tab_s = jnp.tile(_rope_tables(past + jnp.arange(dec, dtype=jnp.int32)), (tm_s // dec, 1))
    page_table = page_table.astype(jnp.int32)
    hp = x_prompt.reshape(np_, d)
    hs = x_sample.reshape(ns_, d)
    nsa_p_l, nsa_s_l, mla_p_l, mla_s_l, win_p_l, win_s_l = [], [], [], [], [], []
    for l in range(depth):
        w1, w_ga, w_gb, wq2, wuk = _arrange_proj_weights(w_in[l], mla_w_uq[l], mla_w_uk[l])
        pe, cw1, cw2 = _arrange_compress_weights(cmp_pe_k[l], cmp_pe_v[l], cmp_k_w1[l], cmp_k_w2[l],
                                                 cmp_v_w1[l], cmp_v_w2[l])
        wuv = _arrange_wuv(mla_w_uv[l])
        tw = _tail_weights((w_ga, w_gb), w_up_a[l], w_up_b[l], w_o[l], ln1_g[l], ln1_b[l], w_group[l], b_group[l],
                           w_router[l], b_router[l], exp_w_gate[l], exp_w_up[l], exp_w_down[l],
                           ln2_g[l], ln2_b[l], w_pe[l], w_pg[l])
        qn, kvn = mla_q_norm[l][None], mla_kv_norm[l][None]
        qw, nsa, kcmp, vcmp, selbf, win, winbf, gt, qmla, mla, kmla = _project(hp, tab_p, s // tm_p, w1, qn, kvn, wq2,
                                                                                wuk, tm=tm_p)
        r3 = lambda a: a.reshape(b, s, -1)
        kcvc = _compress_prompt(r3(kcmp), r3(vcmp), pe, cw1, cw2)
        on = _nsa_prompt(r3(qw), r3(gt), kcvc, r3(selbf), r3(winbf)).reshape(np_, -1)
        om = _mla_prompt(r3(qmla), r3(kmla), wuv).reshape(np_, -1)
        hp = _layer_tail(hp, on, om, p_prompt[l].reshape(np_, -1), tw, alpha, tm=tm_p)
        nsa_p_l.append(nsa.reshape(b, s, 4, NSA_GROUPS, NSA_DH))
        mla_p_l.append(r3(mla))
        win_p_l.append(win.reshape(b, s, 2, NSA_GROUPS, NSA_DH)[:, -min(WINDOW, s):])
        qw, nsa, kcmp, vcmp, selbf, win, winbf, gt, qmla, mla, kmla = _project(hs, tab_s, 1, w1, qn, kvn, wq2, wuk,
                                                                                tm=tm_s)
        r3 = lambda a: a.reshape(db, dec, -1)
        pool_nsa = jnp.transpose(cache_nsa_kv[l], (0, 2, 3, 4, 1)).reshape(n_pool, 4 * KVW, page)
        pool_mla = jnp.transpose(cache_mla[l], (0, 2, 1))
        win_old = jnp.transpose(cache_nsa_win[l], (0, 2, 3, 4, 1)).reshape(db, 2 * KVW, wb)
        o_c, sel = _nsa_sample_cmp(page_table, pool_nsa, r3(qw), pe, cw1, cw2, past)
        on = _nsa_sample_sel(page_table, pool_nsa, r3(qw), r3(gt), o_c, sel, r3(nsa)[:, :, 2 * KVW:], win_old, r3(win),
                             past).reshape(ns_, -1)
        om = _mla_sample(page_table, pool_mla, r3(qmla), r3(mla), wuv).reshape(ns_, -1)
        hs = _layer_tail(hs, on, om, p_sample[l].reshape(ns_, -1), tw, alpha, tm=tm_s)
        nsa_s_l.append(nsa.reshape(db, dec, 4, NSA_GROUPS, NSA_DH))
        mla_s_l.append(r3(mla))
        win_new = win.reshape(db, dec, 2, NSA_GROUPS, NSA_DH)
        win_s_l.append(jnp.concatenate([cache_nsa_win[l], win_new], axis=1)[:, -wb:])
    return (hp.reshape(b, s, d), hs.reshape(db, dec, d), jnp.stack(nsa_p_l), jnp.stack(nsa_s_l),
            jnp.stack(mla_p_l), jnp.stack(mla_s_l), jnp.stack(win_p_l), jnp.stack(win_s_l))
```

```python
import functools

import numpy as np
import jax
import jax.numpy as jnp
from jax import lax
from jax.experimental import pallas as pl
from jax.experimental.pallas import tpu as pltpu

F32 = jnp.float32
BF = jnp.bfloat16

D_MODEL = 1024
NSA_HEADS = 8
NSA_GROUPS = 2
NSA_HPG = NSA_HEADS // NSA_GROUPS
NSA_DH = 64
CMP_BLOCK = 32
CMP_STRIDE = 16
CMP_HIDDEN = 64
SEL_BLOCK = 64
SEL_TOPK = 16
WINDOW = 512
FORCE_BONUS = 1000.0
MLA_HEADS = 8
MLA_Q_RANK = 256
MLA_KV_RANK = 128
MLA_NOPE = 64
MLA_ROPE = 32
MLA_V = 64
ROPE_THETA = 10000.0
N_GROUPS = 4
EXPERTS_PER_GROUP = 8
N_EXPERTS = N_GROUPS * EXPERTS_PER_GROUP
D_EXPERT = 256
MOE_BLOCK = 128
PLE_DIM = 256
Q_BLOCK = 128
LN_EPS = 1e-5
RMS_EPS = 1e-6
KVW = NSA_GROUPS * NSA_DH
LANES = 128
HEAD_LANES = 128
MLA_LANES = 256
NSA_SCALE = NSA_DH ** -0.5
MLA_SCALE = (MLA_NOPE + MLA_ROPE) ** -0.5
NEG_INF = float("-inf")

VMEM_LIMIT = 48 * 1024 * 1024


def _cparams(sem):
    return pltpu.CompilerParams(dimension_semantics=sem, vmem_limit_bytes=VMEM_LIMIT)


def _dot(a, b):
    return jnp.dot(a, b, preferred_element_type=F32)


def _dot_nt(a, b):
    return lax.dot_general(a, b, (((1,), (1,)), ((), ())), preferred_element_type=F32)


def _rms(x, g):
    return x * lax.rsqrt(jnp.mean(x * x, axis=-1, keepdims=True) + RMS_EPS) * g


def _ln(x, g, b):
    mu = jnp.mean(x, axis=-1, keepdims=True)
    var = jnp.mean(jnp.square(x - mu), axis=-1, keepdims=True)
    return (x - mu) * lax.rsqrt(var + LN_EPS) * g + b


_C_QW = 0
_C_NSA = _C_QW + NSA_HEADS * HEAD_LANES
_C_WIN = _C_NSA + 4 * KVW
_C_GT = _C_WIN + 2 * KVW
_C_DQ = _C_GT + LANES
_C_DKV = _C_DQ + MLA_Q_RANK
_C_KR = _C_DKV + MLA_KV_RANK
_C_KRS = _C_KR + LANES
_C_END = _C_KRS + LANES
_Q2_NOPE = MLA_HEADS * MLA_NOPE
_Q2_R1 = _Q2_NOPE
_Q2_R2 = _Q2_R1 + MLA_HEADS * LANES
_Q2_END = _Q2_R2 + MLA_HEADS * LANES


def _proj_kernel(x_ref, w1_ref, tab_ref, qn_ref, kvn_ref, wq2_ref, wuk_ref,
                 qw_ref, nsa_ref, kcmp_ref, vcmp_ref, selbf_ref, win_ref, winbf_ref, gates_ref, qmla_ref, mla_ref,
                 kmla_ref):
    z = _dot(x_ref[...].astype(BF), w1_ref[...])
    qw_ref[...] = (z[:, _C_QW:_C_NSA] * NSA_SCALE).astype(BF)
    nsa = z[:, _C_NSA:_C_WIN]
    nsa_ref[...] = nsa
    kcmp_ref[...] = nsa[:, 0:KVW]
    vcmp_ref[...] = nsa[:, KVW:2 * KVW]
    selbf_ref[...] = nsa[:, 2 * KVW:].astype(BF)
    win = z[:, _C_WIN:_C_GT]
    win_ref[...] = win
    winbf_ref[...] = win.astype(BF)
    gates_ref[...] = z[:, _C_GT:_C_DQ]
    rn = _rms(z[:, _C_DQ:_C_DKV], qn_ref[...])
    r = _dot(rn.astype(BF), wq2_ref[...])
    lat = _dot(r[:, :_Q2_NOPE].astype(BF), wuk_ref[...])
    tab = tab_ref[...]
    c1, c2 = tab[:, 0:LANES], tab[:, LANES:2 * LANES]
    ck, sk = tab[:, 2 * LANES:3 * LANES], tab[:, 3 * LANES:4 * LANES]
    for h in range(MLA_HEADS):
        base = h * MLA_LANES
        qmla_ref[:, base:base + LANES] = lat[:, h * LANES:(h + 1) * LANES].astype(BF)
        pe = (r[:, _Q2_R1 + h * LANES:_Q2_R1 + (h + 1) * LANES] * c1
              + r[:, _Q2_R2 + h * LANES:_Q2_R2 + (h + 1) * LANES] * c2)
        qmla_ref[:, base + LANES:base + MLA_LANES] = pe.astype(BF)
    ckv = _rms(z[:, _C_DKV:_C_KR], kvn_ref[...])
    kpe = z[:, _C_KR:_C_KRS] * ck + z[:, _C_KRS:_C_END] * sk
    mla_ref[:, 0:MLA_KV_RANK] = ckv
    mla_ref[:, MLA_KV_RANK:MLA_KV_RANK + MLA_ROPE] = kpe[:, 0:MLA_ROPE]
    kmla_ref[:, 0:LANES] = ckv.astype(BF)
    ones_lane = lax.broadcasted_iota(jnp.int32, kpe.shape, 1) == MLA_ROPE
    kmla_ref[:, LANES:MLA_LANES] = jnp.where(ones_lane, 1.0, kpe).astype(BF)


def _rope_tables(pos):
    half = MLA_ROPE // 2
    inv = ROPE_THETA ** (-jnp.arange(half, dtype=F32) / half)
    ang = pos.astype(F32)[:, None] * inv
    cos, sin = jnp.cos(ang), jnp.sin(ang)
    pad = jnp.zeros((pos.shape[0], LANES - MLA_ROPE), F32)
    return jnp.concatenate([cos, sin, pad, -sin, cos, pad, cos, cos, pad, -sin, sin, pad], axis=1)


def _arrange_proj_weights(w_in, w_uq, w_uk):
    d = w_in.shape[0]
    o = np.cumsum((0, NSA_HEADS * NSA_DH, KVW, KVW, KVW, KVW, KVW, KVW, 3 * NSA_HEADS,
                   MLA_Q_RANK, MLA_KV_RANK, MLA_ROPE, D_MODEL, D_MODEL))
    zeros = lambda n: jnp.zeros((d, n), w_in.dtype)
    cols = []
    for c in range(NSA_HEADS):
        wq = w_in[:, c * NSA_DH:(c + 1) * NSA_DH]
        cols += [wq, zeros(NSA_DH)] if c < NSA_HPG else [zeros(NSA_DH), wq]
    cols.append(w_in[:, o[1]:o[7]])
    cols += [w_in[:, o[7]:o[8]], zeros(LANES - 3 * NSA_HEADS)]
    cols.append(w_in[:, o[8]:o[10]])
    kr = w_in[:, o[10]:o[11]]
    half = MLA_ROPE // 2
    cols += [kr, zeros(LANES - MLA_ROPE), kr[:, half:], kr[:, :half], zeros(LANES - MLA_ROPE)]
    w1 = jnp.concatenate(cols, axis=1).astype(BF)
    w_ga = w_in[:, o[11]:o[12]].astype(BF)
    w_gb = w_in[:, o[12]:o[13]].astype(BF)

    uq = w_uq.reshape(MLA_Q_RANK, MLA_HEADS, MLA_NOPE + MLA_ROPE)
    zq = jnp.zeros((MLA_Q_RANK, LANES - MLA_ROPE), w_uq.dtype)
    q2 = [uq[:, :, :MLA_NOPE].reshape(MLA_Q_RANK, -1)]
    for lo in (MLA_NOPE, MLA_NOPE + half):
        for h in range(MLA_HEADS):
            x = uq[:, h, lo:lo + half]
            q2 += [x, x, zq]
    wq2 = jnp.concatenate(q2, axis=1).astype(BF)
    wuk = jnp.zeros((MLA_HEADS * MLA_NOPE, MLA_HEADS * LANES), w_uk.dtype)
    for h in range(MLA_HEADS):
        wuk = wuk.at[h * MLA_NOPE:(h + 1) * MLA_NOPE, h * LANES:(h + 1) * LANES].set(w_uk[:, h, :].T)
    return w1, w_ga, w_gb, wq2, wuk.astype(BF)


def _project(x2d, tab, tab_blocks, w1, qn, kvn, wq2, wuk, tm=256):
    n = x2d.shape[0]
    grid = (n // tm,)
    row = lambda w: pl.BlockSpec((tm, w), lambda i: (i, 0))
    full = lambda a: pl.BlockSpec(a.shape, lambda i: (0,) * a.ndim)
    outs = [(NSA_HEADS * HEAD_LANES, BF), (4 * KVW, F32), (KVW, F32), (KVW, F32), (2 * KVW, BF), (2 * KVW, F32),
            (2 * KVW, BF),
            (LANES, F32), (MLA_HEADS * MLA_LANES, BF), (MLA_KV_RANK + MLA_ROPE, F32), (MLA_LANES, BF)]
    return pl.pallas_call(
        _proj_kernel,
        grid=grid,
        in_specs=[row(D_MODEL), full(w1), pl.BlockSpec((tm, 4 * LANES), lambda i: (i % tab_blocks, 0)),
                  full(qn), full(kvn), full(wq2), full(wuk)],
        out_specs=[row(w) for w, _ in outs],
        out_shape=[jax.ShapeDtypeStruct((n, w), dt) for w, dt in outs],
        compiler_params=_cparams(("parallel",)),
        name="proj",
    )(x2d, w1, tab, qn, kvn, wq2, wuk)


def _compress_partial(chunk_rows, pe_ref, w1_ref):
    a = b = None
    for l in range(CMP_STRIDE):
        xl = chunk_rows(l)
        ta = _dot((xl + pe_ref[l:l + 1, :]).astype(BF), w1_ref[l])
        tb = _dot((xl + pe_ref[CMP_STRIDE + l:CMP_STRIDE + l + 1, :]).astype(BF), w1_ref[CMP_STRIDE + l])
        a = ta if a is None else a + ta
        b = tb if b is None else b + tb
    return a, b


def _compress_kernel(k_ref, v_ref, pe_ref, w1_ref, w2_ref, o_ref, b_scr):
    nchunk = o_ref.shape[1]
    a, b = _compress_partial(
        lambda l: jnp.concatenate([k_ref[0, pl.ds(l, nchunk, stride=CMP_STRIDE), :],
                                   v_ref[0, pl.ds(l, nchunk, stride=CMP_STRIDE), :]], axis=1), pe_ref, w1_ref)
    b_scr[0:nchunk, :] = b
    b_scr[nchunk:nchunk + 8, :] = jnp.zeros((8, 4 * NSA_DH), F32)
    pre = a + b_scr[pl.ds(1, nchunk), :]
    o_ref[0] = _dot(jax.nn.gelu(pre).astype(BF), w2_ref[...]).astype(BF)


def _arrange_compress_weights(pe_k, pe_v, k_w1, k_w2, v_w1, v_w2):
    dh, hid = NSA_DH, CMP_HIDDEN
    pe = jnp.concatenate([pe_k, pe_k, pe_v, pe_v], axis=1)
    w1k = k_w1.reshape(CMP_BLOCK, dh, hid)
    w1v = v_w1.reshape(CMP_BLOCK, dh, hid)
    w1 = jnp.zeros((CMP_BLOCK, 4 * dh, 4 * hid), k_w1.dtype)
    w2 = jnp.zeros((4 * hid, 4 * dh), k_w2.dtype)
    for blk, (a1, a2) in enumerate(((w1k, k_w2), (w1k, k_w2), (w1v, v_w2), (w1v, v_w2))):
        w1 = w1.at[:, blk * dh:(blk + 1) * dh, blk * hid:(blk + 1) * hid].set(a1)
        w2 = w2.at[blk * hid:(blk + 1) * hid, blk * dh:(blk + 1) * dh].set(a2)
    return pe, w1.astype(BF), w2.astype(BF)


def _compress_prompt(kcmp_rows, vcmp_rows, pe, w1, w2):
    b, s, _ = kcmp_rows.shape
    nchunk = s // CMP_STRIDE
    full = lambda a: pl.BlockSpec(a.shape, lambda i: (0,) * a.ndim)
    rows = pl.BlockSpec((1, s, KVW), lambda i: (i, 0, 0))
    return pl.pallas_call(
        _compress_kernel,
        grid=(b,),
        in_specs=[rows, rows, full(pe), full(w1), full(w2)],
        out_specs=pl.BlockSpec((1, nchunk, 4 * NSA_DH), lambda i: (i, 0, 0)),
        out_shape=jax.ShapeDtypeStruct((b, nchunk, 4 * NSA_DH), BF),
        scratch_shapes=[pltpu.VMEM((nchunk + 8, 4 * NSA_DH), F32)],
        compiler_params=_cparams(("parallel",)),
        name="compress_prompt",
    )(kcmp_rows, vcmp_rows, pe, w1, w2)


def _masked_softmax(s, mask):
    s = jnp.where(mask, s, NEG_INF)
    m = jnp.max(s, axis=-1, keepdims=True)
    m = jnp.where(m == NEG_INF, 0.0, m)
    e = jnp.exp(s - m)
    den = jnp.sum(e, axis=-1, keepdims=True)
    return e / jnp.where(den > 0, den, 1.0)


def _online_update(s, v, m_ref, l_ref, acc_ref, v_transposed=False):
    m_old = m_ref[...]
    m_new = jnp.maximum(m_old, jnp.max(s, axis=-1, keepdims=True))
    m_safe = jnp.where(m_new == NEG_INF, 0.0, m_new)
    alpha = jnp.exp(m_old - m_safe)
    e = jnp.exp(s - m_safe)
    l_ref[...] = alpha * l_ref[...] + jnp.sum(e, axis=-1, keepdims=True)
    pv = _dot_nt(e.astype(BF), v) if v_transposed else _dot(e.astype(BF), v)
    acc_ref[...] = alpha * acc_ref[...] + pv
    m_ref[...] = m_new


def _online_init(m_ref, l_ref, acc_ref):
    m_ref[...] = jnp.full(m_ref.shape, NEG_INF, F32)
    l_ref[...] = jnp.zeros(l_ref.shape, F32)
    acc_ref[...] = jnp.zeros(acc_ref.shape, F32)


def _online_result(l_ref, acc_ref):
    l = l_ref[...]
    return acc_ref[...] / jnp.where(l > 0, l, 1.0)


def _alibi_slope_rows(g, rows_per_head):
    hh = lax.broadcasted_iota(jnp.int32, (NSA_HPG * rows_per_head, 1), 0) // rows_per_head
    slope = jnp.zeros(hh.shape, F32)
    for h in range(NSA_HPG):
        slope = jnp.where(hh == h, 2.0 ** -(g * NSA_HPG + h + 1), slope)
    return slope


def _select_topk(score, n_lanes):
    j = lax.broadcasted_iota(jnp.int32, score.shape, 1)
    sel = jnp.zeros(score.shape, F32)
    for _ in range(SEL_TOPK):
        m = jnp.max(score, axis=-1, keepdims=True)
        idx = jnp.min(jnp.where(score == m, j, n_lanes), axis=-1, keepdims=True)
        pick = (j == idx) & (m > NEG_INF)
        sel = jnp.where(pick, 1.0, sel)
        score = jnp.where(pick, NEG_INF, score)
    return sel


def _split_dot(x, w):
    hi = x.astype(BF)
    lo = (x - hi.astype(F32)).astype(BF)
    return _dot(hi, w) + _dot(lo, w)


SEL_TILE = 2048


def _nsa_prompt_kernel(qw_ref, gt_ref, kcvc_ref, sel_ref, win_ref, ovl_ref, o_ref, *scratch, nc):
    tq = qw_ref.shape[1]
    ncp = kcvc_ref.shape[1]
    q0 = pl.program_id(1) * tq
    rows = NSA_HPG * tq
    m_refs, l_refs, acc_refs = scratch[0:2], scratch[2:4], scratch[4:6]
    qi = lax.broadcasted_iota(jnp.int32, (rows, 1), 0) % tq
    q_pos = q0 + qi
    qp1 = q0 + lax.broadcasted_iota(jnp.int32, (tq, 1), 0)
    kc = kcvc_ref[0, :, 0:KVW]
    vc = kcvc_ref[0, :, KVW:2 * KVW]
    gates = jax.nn.sigmoid(gt_ref[0])
    wstart = pl.multiple_of(jnp.maximum(q0 - WINDOW, 0), tq)
    wk = WINDOW + tq
    kw = win_ref[0, pl.ds(wstart, wk), 0:KVW]
    vw = win_ref[0, pl.ds(wstart, wk), KVW:2 * KVW]
    jblk = lax.broadcasted_iota(jnp.int32, (tq, LANES), 1)
    c = lax.broadcasted_iota(jnp.int32, (1, ncp), 1)
    c_end = c * CMP_STRIDE + CMP_BLOCK - 1
    rel_c = (c_end - q0).astype(F32)
    wpos = wstart + lax.broadcasted_iota(jnp.int32, (1, wk), 1)
    rel_w = (wpos - q0).astype(F32)
    dist_w = q_pos - wpos

    qs, slopes, o_cs, o_ws, scores = [], [], [], [], []
    for g in range(NSA_GROUPS):
        q = jnp.concatenate([qw_ref[0, :, (g * NSA_HPG + h) * HEAD_LANES:(g * NSA_HPG + h + 1) * HEAD_LANES]
                             for h in range(NSA_HPG)], axis=0)
        slope = _alibi_slope_rows(g, tq)
        qs.append(q)
        slopes.append(slope)
        p_c = _masked_softmax(_dot_nt(q, kc) + slope * rel_c, (c_end <= q_pos) & (c < nc))
        o_cs.append(_dot(p_c.astype(BF), vc))
        p_sum = p_c[0:tq]
        for h in range(1, NSA_HPG):
            p_sum = p_sum + p_c[h * tq:(h + 1) * tq]
        imp = _split_dot(p_sum, ovl_ref[...])
        cur = qp1 // SEL_BLOCK
        avail = jblk * SEL_BLOCK <= qp1
        forced = (jblk == 0) | (jblk == cur) | (jblk == cur - 1)
        scores.append(jnp.where(avail, imp + FORCE_BONUS * forced.astype(F32), NEG_INF))
        p_w = _masked_softmax(_dot_nt(q, kw) + slope * rel_w, (dist_w >= 0) & (dist_w < WINDOW))
        o_ws.append(_dot(p_w.astype(BF), vw))
        _online_init(m_refs[g], l_refs[g], acc_refs[g])
    sel_all = _select_topk(jnp.concatenate(scores, axis=0), LANES).astype(BF)
    sels = [sel_all[g * tq:(g + 1) * tq] for g in range(NSA_GROUPS)]

    def tile(t, causal):
        k0 = pl.multiple_of(t * SEL_TILE, SEL_TILE)
        kk = sel_ref[0, pl.ds(k0, SEL_TILE), 0:KVW]
        vv = sel_ref[0, pl.ds(k0, SEL_TILE), KVW:2 * KVW]
        kpos = k0 + lax.broadcasted_iota(jnp.int32, (1, SEL_TILE), 1)
        rel = (kpos - q0).astype(F32)
        expand = lax.broadcasted_iota(jnp.int32, (LANES, SEL_TILE), 0) == kpos // SEL_BLOCK
        expand = jnp.where(expand, 1.0, 0.0).astype(BF)
        for g in range(NSA_GROUPS):
            picked = _dot(sels[g], expand) > 0.5
            s = (_dot_nt(qs[g], kk) + slopes[g] * rel).reshape(NSA_HPG, tq, SEL_TILE)
            s = jnp.where(picked[None], s, NEG_INF).reshape(rows, SEL_TILE)
            if causal:
                s = jnp.where(kpos <= q_pos, s, NEG_INF)
            _online_update(s, vv, m_refs[g], l_refs[g], acc_refs[g])

    def step(t, carry):
        tile(t, False)
        return carry

    n_full = q0 // SEL_TILE
    lax.fori_loop(0, n_full, step, 0)
    tile(n_full, True)
    for g in range(NSA_GROUPS):
        o_s = _online_result(l_refs[g], acc_refs[g])
        for h in range(NSA_HPG):
            head = g * NSA_HPG + h
            r = slice(h * tq, (h + 1) * tq)
            out = (gates[:, 3 * head:3 * head + 1] * o_cs[g][r]
                   + gates[:, 3 * head + 1:3 * head + 2] * o_s[r]
                   + gates[:, 3 * head + 2:3 * head + 3] * o_ws[g][r])
            o_ref[0, :, head * HEAD_LANES:(head + 1) * HEAD_LANES] = out.astype(BF)


def _overlap_matrix(ncp, nc):
    c0 = np.arange(ncp)[:, None] * CMP_STRIDE
    c1 = c0 + CMP_BLOCK - 1
    s0 = np.arange(LANES)[None, :] * SEL_BLOCK
    s1 = s0 + SEL_BLOCK - 1
    ok = (c0 <= s1) & (c1 >= s0) & (np.arange(ncp)[:, None] < nc)
    return jnp.asarray(ok.astype(np.float32)).astype(BF)


def _nsa_prompt(qw, gates, kcvc, sel_bf, win_bf):
    b, s, _ = qw.shape
    tq = Q_BLOCK
    ncp = kcvc.shape[1]
    nc = (s - CMP_BLOCK) // CMP_STRIDE + 1
    ovl = _overlap_matrix(ncp, nc)
    rows = NSA_HPG * tq
    return pl.pallas_call(
        functools.partial(_nsa_prompt_kernel, nc=nc),
        grid=(b, s // tq),
        in_specs=[pl.BlockSpec((1, tq, NSA_HEADS * HEAD_LANES), lambda bi, i: (bi, i, 0)),
                  pl.BlockSpec((1, tq, LANES), lambda bi, i: (bi, i, 0)),
                  pl.BlockSpec((1, ncp, 2 * KVW), lambda bi, i: (bi, 0, 0)),
                  pl.BlockSpec((1, s, 2 * KVW), lambda bi, i: (bi, 0, 0)),
                  pl.BlockSpec((1, s, 2 * KVW), lambda bi, i: (bi, 0, 0)),
                  pl.BlockSpec(ovl.shape, lambda bi, i: (0, 0))],
        out_specs=pl.BlockSpec((1, tq, NSA_HEADS * HEAD_LANES), lambda bi, i: (bi, i, 0)),
        out_shape=jax.ShapeDtypeStruct((b, s, NSA_HEADS * HEAD_LANES), BF),
        scratch_shapes=([pltpu.VMEM((rows, 1), F32)] * (2 * NSA_GROUPS) + [pltpu.VMEM((rows, KVW), F32)] * NSA_GROUPS),
        compiler_params=_cparams(("parallel", "arbitrary")),
        name="nsa_prompt",
    )(qw, gates, kcvc, sel_bf, win_bf, ovl)


MLA_TILE = 2048


MLA_ONES_LANE = MLA_KV_RANK + MLA_ROPE


MLA_CHAINS = 4


def _mla_prompt_kernel(q_ref, k_ref, wuv_ref, o_ref, *scratch):
    tq = q_ref.shape[1]
    q0 = pl.program_id(1) * tq
    hpc = MLA_HEADS // MLA_CHAINS
    rows = hpc * tq
    m_refs, acc_refs = scratch[:MLA_CHAINS], scratch[MLA_CHAINS:]
    qs = [jnp.concatenate([q_ref[0, :, h * MLA_LANES:(h + 1) * MLA_LANES] for h in range(c * hpc, (c + 1) * hpc)],
                          axis=0) for c in range(MLA_CHAINS)]
    q_pos = q0 + lax.broadcasted_iota(jnp.int32, (rows, 1), 0) % tq
    for c in range(MLA_CHAINS):
        m_refs[c][...] = jnp.full((rows, 1), NEG_INF, F32)
        acc_refs[c][...] = jnp.zeros((rows, MLA_LANES), F32)

    def tile(t, causal):
        k0 = pl.multiple_of(t * MLA_TILE, MLA_TILE)
        kk = k_ref[0, pl.ds(k0, MLA_TILE), :]
        for c in range(MLA_CHAINS):
            s = _dot_nt(qs[c], kk) * MLA_SCALE
            if causal:
                kpos = k0 + lax.broadcasted_iota(jnp.int32, (1, MLA_TILE), 1)
                s = jnp.where(kpos <= q_pos, s, NEG_INF)
            m_old = m_refs[c][...]
            m_new = jnp.maximum(m_old, jnp.max(s, axis=-1, keepdims=True))
            m_refs[c][...] = m_new
            e = jnp.exp(s - m_new)
            acc_refs[c][...] = jnp.exp(m_old - m_new) * acc_refs[c][...] + _dot(e.astype(BF), kk)

    def step(t, carry):
        tile(t, False)
        return carry

    n_full = q0 // MLA_TILE
    lax.fori_loop(0, n_full, step, 0)
    tile(n_full, True)
    out = None
    for c in range(MLA_CHAINS):
        acc = acc_refs[c][...]
        o_lat = (acc[:, 0:MLA_KV_RANK] / acc[:, MLA_ONES_LANE:MLA_ONES_LANE + 1]).astype(BF)
        for i in range(hpc):
            part = _dot(o_lat[i * tq:(i + 1) * tq], wuv_ref[c * hpc + i])
            out = part if out is None else out + part
    o_ref[0] = out.astype(BF)


def _arrange_wuv(w_uv):
    w = jnp.zeros((MLA_HEADS, MLA_KV_RANK, MLA_HEADS * MLA_V), w_uv.dtype)
    for h in range(MLA_HEADS):
        w = w.at[h, :, h * MLA_V:(h + 1) * MLA_V].set(w_uv[:, h, :])
    return w.astype(BF)


def _mla_prompt(qmla, kmla, wuv):
    b, s, _ = qmla.shape
    tq = Q_BLOCK
    rows = MLA_HEADS * tq
    return pl.pallas_call(
        _mla_prompt_kernel,
        grid=(b, s // tq),
        in_specs=[pl.BlockSpec((1, tq, MLA_HEADS * MLA_LANES), lambda bi, i: (bi, i, 0)),
                  pl.BlockSpec((1, s, MLA_LANES), lambda bi, i: (bi, 0, 0)),
                  pl.BlockSpec(wuv.shape, lambda bi, i: (0, 0, 0))],
        out_specs=pl.BlockSpec((1, tq, MLA_HEADS * MLA_V), lambda bi, i: (bi, i, 0)),
        out_shape=jax.ShapeDtypeStruct((b, s, MLA_HEADS * MLA_V), BF),
        scratch_shapes=([pltpu.VMEM((rows // MLA_CHAINS, 1), F32)] * MLA_CHAINS
                        + [pltpu.VMEM((rows // MLA_CHAINS, MLA_LANES), F32)] * MLA_CHAINS),
        compiler_params=_cparams(("parallel", "arbitrary")),
        name="mla_prompt",
    )(qmla, kmla, wuv)


_R_EXPERT0 = 32


def _merge_kernel(x_ref, on_ref, om_ref, wga_ref, wgb_ref, wa_ref, wb_ref, wo_ref, g1_ref, b1_ref,
                  wr_ref, br_ref, h_ref, route_ref, *, alpha):
    x = x_ref[...]
    xb = x.astype(BF)
    ga = jax.nn.sigmoid(_dot(xb, wga_ref[...]))
    gb = jax.nn.sigmoid(_dot(xb, wgb_ref[...]))
    mix = ga * _dot(on_ref[...], wa_ref[...]) + gb * _dot(om_ref[...], wb_ref[...])
    h = _ln(alpha * x + _dot(mix.astype(BF), wo_ref[...]), g1_ref[...], b1_ref[...])
    h_ref[...] = h
    logit = _dot(h.astype(BF), wr_ref[...]) + br_ref[...]
    lane = lax.broadcasted_iota(jnp.int32, logit.shape, 1)
    is_g = lane < N_GROUPS
    gl = jnp.where(is_g, logit, NEG_INF)
    gmax = jnp.max(gl, axis=-1, keepdims=True)
    gsel = jnp.min(jnp.where(gl == gmax, lane, LANES), axis=-1, keepdims=True)
    g_w = 1.0 / jnp.sum(jnp.exp(gl - gmax), axis=-1, keepdims=True)
    e_lane = lane - _R_EXPERT0
    in_grp = (e_lane >= 0) & (e_lane < N_EXPERTS) & (e_lane // EXPERTS_PER_GROUP == gsel)
    el = jnp.where(in_grp, logit, NEG_INF)
    v1 = jnp.max(el, axis=-1, keepdims=True)
    i1 = jnp.min(jnp.where(el == v1, e_lane, LANES), axis=-1, keepdims=True)
    el2 = jnp.where(e_lane == i1, NEG_INF, el)
    v2 = jnp.max(el2, axis=-1, keepdims=True)
    i2 = jnp.min(jnp.where(el2 == v2, e_lane, LANES), axis=-1, keepdims=True)
    e2 = jnp.exp(v2 - v1)
    den = 1.0 + e2
    route = jnp.where(lane == 0, i1.astype(F32),
                      jnp.where(lane == 1, i2.astype(F32),
                                jnp.where(lane == 2, g_w * (1.0 / den),
                                          jnp.where(lane == 3, g_w * (e2 / den), 0.0))))
    route_ref[...] = route


def _arrange_merge_weights(w_up_a, w_group, b_group, w_router, b_router):
    wa = jnp.zeros((NSA_HEADS * HEAD_LANES, D_MODEL), w_up_a.dtype)
    for c in range(NSA_HEADS):
        off = c * HEAD_LANES + (0 if c < NSA_HPG else NSA_DH)
        wa = wa.at[off:off + NSA_DH].set(w_up_a[c * NSA_DH:(c + 1) * NSA_DH])
    d = w_group.shape[0]
    wr = jnp.concatenate([w_group, jnp.zeros((d, _R_EXPERT0 - N_GROUPS), w_group.dtype), w_router,
                          jnp.zeros((d, LANES - _R_EXPERT0 - N_EXPERTS), w_group.dtype)], axis=1)
    br = jnp.concatenate([b_group, jnp.zeros((_R_EXPERT0 - N_GROUPS,), b_group.dtype), b_router,
                          jnp.zeros((LANES - _R_EXPERT0 - N_EXPERTS,), b_group.dtype)])[None]
    return wa.astype(BF), wr.astype(BF), br


def _merge(x2d, on, om, w_ga, w_gb, wa, wb, wo, g1, b1, wr, br, alpha, tm=256):
    n = x2d.shape[0]
    row = lambda w: pl.BlockSpec((tm, w), lambda i: (i, 0))
    full = lambda a: pl.BlockSpec(a.shape, lambda i: (0,) * a.ndim)
    consts = (w_ga, w_gb, wa, wb, wo, g1, b1, wr, br)
    return pl.pallas_call(
        functools.partial(_merge_kernel, alpha=alpha),
        grid=(n // tm,),
        in_specs=[row(D_MODEL), row(on.shape[1]), row(om.shape[1])] + [full(a) for a in consts],
        out_specs=[row(D_MODEL), row(LANES)],
        out_shape=[jax.ShapeDtypeStruct((n, D_MODEL), F32), jax.ShapeDtypeStruct((n, LANES), F32)],
        compiler_params=_cparams(("parallel",)),
        name="merge_ln1_route",
    )(x2d, on, om, *consts)


def _expert_kernel(blk_e_ref, src_ref, dst_ref, h_hbm, wg_ref, wu_ref, wd_ref, y_hbm, xbuf, ybuf, gsem, ssem):
    i = pl.program_id(0)
    nb = pl.num_programs(0)
    slot = i % 2
    d = xbuf.shape[2]

    def gather(blk, sl, r):
        return pltpu.make_async_copy(h_hbm.at[pl.ds(src_ref[blk * MOE_BLOCK + r], 1)], xbuf.at[sl, pl.ds(r, 1)],
                                     gsem.at[sl])

    def scatter(blk, sl, r):
        dst = dst_ref[blk * MOE_BLOCK + r]
        col = pl.multiple_of((dst & 1) * d, d)
        return pltpu.make_async_copy(ybuf.at[sl, pl.ds(r, 1)],
                                     y_hbm.at[pl.ds(lax.shift_right_logical(dst, 1), 1), pl.ds(col, d)], ssem.at[sl])

    def for_rows(fn):
        def body(r, c):
            fn(r)
            return c
        lax.fori_loop(0, MOE_BLOCK, body, 0, unroll=8)

    @pl.when(i == 0)
    def _():
        for_rows(lambda r: gather(i, slot, r).start())

    @pl.when(i + 1 < nb)
    def _():
        for_rows(lambda r: gather(i + 1, 1 - slot, r).start())

    for_rows(lambda r: gather(i, slot, r).wait())

    @pl.when(i >= 2)
    def _():
        for_rows(lambda r: scatter(i - 2, slot, r).wait())

    xb = xbuf[slot].astype(BF)
    hid = jax.nn.silu(_dot(xb, wg_ref[0])) * _dot(xb, wu_ref[0])
    ybuf[slot] = _dot(hid.astype(BF), wd_ref[0])
    for_rows(lambda r: scatter(i, slot, r).start())

    @pl.when(i == nb - 1)
    def _():
        for_rows(lambda r: scatter(i - 1, 1 - slot, r).wait())
        for_rows(lambda r: scatter(i, slot, r).wait())


def _experts(h, blk_e, src, dst, wg, wu, wd):
    n, d = h.shape
    nb = blk_e.shape[0]
    assert nb >= 2
    wspec = lambda a: pl.BlockSpec((1,) + a.shape[1:], lambda i, e, s, t: (e[i], 0, 0))
    return pl.pallas_call(
        _expert_kernel,
        grid_spec=pltpu.PrefetchScalarGridSpec(
            num_scalar_prefetch=3,
            grid=(nb,),
            in_specs=[pl.BlockSpec(memory_space=pl.ANY), wspec(wg), wspec(wu), wspec(wd)],
            out_specs=pl.BlockSpec(memory_space=pl.ANY),
            scratch_shapes=[pltpu.VMEM((2, MOE_BLOCK, d), F32), pltpu.VMEM((2, MOE_BLOCK, d), F32),
                            pltpu.SemaphoreType.DMA((2,)), pltpu.SemaphoreType.DMA((2,))]),
        out_shape=jax.ShapeDtypeStruct((n + MOE_BLOCK, 2 * d), F32),
        compiler_params=_cparams(("arbitrary",)),
        name="experts",
    )(blk_e, src, dst, h, wg, wu, wd)


def _route_plan(route, n):
    eid = route[:, 0:2].astype(jnp.int32).reshape(-1)
    m = eid.shape[0]
    order = jnp.argsort(eid)
    e_sorted = eid[order]
    counts = jnp.bincount(eid, length=N_EXPERTS)
    padded = (counts + MOE_BLOCK - 1) // MOE_BLOCK * MOE_BLOCK
    pad_end = jnp.cumsum(padded)
    start = jnp.cumsum(counts) - counts
    pad_start = pad_end - padded
    nb = -(-m // MOE_BLOCK) + N_EXPERTS
    blk_e = jnp.minimum(jnp.sum(pad_end[None, :] <= (jnp.arange(nb) * MOE_BLOCK)[:, None], axis=1),
                        N_EXPERTS - 1).astype(jnp.int32)
    s_all = jnp.arange(nb * MOE_BLOCK, dtype=jnp.int32)
    e_slot = jnp.repeat(blk_e, MOE_BLOCK)
    within = s_all - pad_start[e_slot].astype(jnp.int32)
    valid = within < counts[e_slot]
    assign = order[jnp.clip(start[e_slot] + within, 0, m - 1)].astype(jnp.int32)
    dump = m + MOE_BLOCK * ((s_all // MOE_BLOCK) % 2) + s_all % MOE_BLOCK
    src = jnp.where(valid, assign // 2, 0)
    dst = jnp.where(valid, assign, dump)
    return blk_e, src, dst


def _final_kernel(h_ref, y_ref, route_ref, g2_ref, b2_ref, wpg_ref, p_ref, wpe_ref, o_ref, *, alpha):
    h = h_ref[...]
    route = route_ref[...]
    moe = route[:, 2:3] * y_ref[:, 0:D_MODEL] + route[:, 3:4] * y_ref[:, D_MODEL:2 * D_MODEL]
    h2 = _ln(alpha * h + moe, g2_ref[...], b2_ref[...])
    gate = jax.nn.sigmoid(_dot(h2.astype(BF), wpg_ref[...]))
    o_ref[...] = h2 + gate * _dot(p_ref[...].astype(BF), wpe_ref[...])


def _final(h, y2, route, g2, b2, wpg, p2d, wpe, alpha, tm=256):
    n = h.shape[0]
    row = lambda w: pl.BlockSpec((tm, w), lambda i: (i, 0))
    full = lambda a: pl.BlockSpec(a.shape, lambda i: (0,) * a.ndim)
    return pl.pallas_call(
        functools.partial(_final_kernel, alpha=alpha),
        grid=(n // tm,),
        in_specs=[row(D_MODEL), row(2 * D_MODEL), row(LANES), full(g2), full(b2), full(wpg), row(PLE_DIM), full(wpe)],
        out_specs=row(D_MODEL),
        out_shape=jax.ShapeDtypeStruct((n, D_MODEL), F32),
        compiler_params=_cparams(("parallel",)),
        name="combine_ln2_ple",
    )(h, y2, route, g2, b2, wpg, p2d, wpe)


def _layer_tail(x2d, on, om, p2d, tw, alpha, tm=256):
    n = x2d.shape[0]
    h, route = _merge(x2d, on, om, tw["w_ga"], tw["w_gb"], tw["wa"], tw["wb"], tw["wo"], tw["g1"], tw["b1"],
                      tw["wr"], tw["br"], alpha, tm=tm)
    blk_e, src, dst = _route_plan(route, n)
    y = _experts(h, blk_e, src, dst, tw["wg"], tw["wu"], tw["wd"])
    return _final(h, y, route, tw["g2"], tw["b2"], tw["wpg"], p2d, tw["wpe"], alpha, tm=tm)


SAMPLE_PAGES_PER_STEP = 64
CMP_PAGES_PER_STEP = 64


def _stack_heads(ref, n_heads, lanes):
    return jnp.concatenate([ref[0, :, c * lanes:(c + 1) * lanes] for c in range(n_heads)], axis=0)


def _alibi_rows(first_head, n_heads, rows_per_head):
    hh = lax.broadcasted_iota(jnp.int32, (n_heads * rows_per_head, 1), 0) // rows_per_head
    slope = jnp.zeros(hh.shape, F32)
    for h in range(n_heads):
        slope = jnp.where(hh == h, 2.0 ** -(first_head + h + 1), slope)
    return slope


def _pad_rows(x, n):
    return jnp.concatenate([x, jnp.zeros((n - x.shape[0], x.shape[1]), x.dtype)], axis=0)


def _nsa_sample_cmp_kernel(pt_ref, qw_ref, pe_ref, w1_ref, w2_ref, ovl_ref, perm_ref, *rest, pps, past, nc):
    pages = rest[:pps]
    oc_ref, sel_ref, a_scr, b_scr, xk_scr, xv_scr = rest[pps:]
    step = pl.program_id(1)
    page = pages[0].shape[2]
    cpp = page // CMP_STRIDE
    nstep = pps * cpp
    perm = perm_ref[...]
    for k, pg in enumerate(pages):
        for off, scr in ((0, xk_scr), (KVW, xv_scr)):
            xt = pg[0, off:off + KVW, :]
            hi = xt.astype(BF)
            lo = (xt - hi.astype(F32)).astype(BF)
            scr[k * page:(k + 1) * page, :] = (_dot(hi, perm) + _dot(lo, perm)).T

    def chunk_rows(l):
        return jnp.concatenate(
            [jnp.concatenate([scr[k * page + l * cpp:k * page + (l + 1) * cpp, :] for k in range(pps)], axis=0)
             for scr in (xk_scr, xv_scr)], axis=1)

    a, b = _compress_partial(chunk_rows, pe_ref, w1_ref)
    r0 = pl.multiple_of(step * nstep, nstep)
    a_scr[pl.ds(r0, nstep), :] = a
    b_scr[pl.ds(r0, nstep), :] = b

    @pl.when(step == pl.num_programs(1) - 1)
    def _():
        _nsa_sample_select(qw_ref, w2_ref, ovl_ref, oc_ref, sel_ref, a_scr, b_scr, past=past, nc=nc)


def _nsa_sample_select(qw_ref, w2_ref, ovl_ref, oc_ref, sel_ref, a_scr, b_scr, *, past, nc):
    dec = qw_ref.shape[1]
    nchunk = a_scr.shape[0]
    nsl = sel_ref.shape[2]
    b_scr[nchunk:nchunk + 8, :] = jnp.zeros((8, 4 * NSA_DH), F32)
    pre = a_scr[...] + b_scr[pl.ds(1, nchunk), :]
    kcvc = _dot(jax.nn.gelu(pre).astype(BF), w2_ref[...]).astype(BF)
    kc = kcvc[:, 0:KVW]
    vc = kcvc[:, KVW:2 * KVW]
    rows = NSA_HPG * dec
    q_pos = past + lax.broadcasted_iota(jnp.int32, (rows, 1), 0) % dec
    qp1 = past + lax.broadcasted_iota(jnp.int32, (dec, 1), 0)
    c = lax.broadcasted_iota(jnp.int32, (1, nchunk), 1)
    jblk = lax.broadcasted_iota(jnp.int32, (dec, nsl), 1)
    for g in range(NSA_GROUPS):
        q = jnp.concatenate([qw_ref[0, :, (g * NSA_HPG + h) * HEAD_LANES:(g * NSA_HPG + h + 1) * HEAD_LANES]
                             for h in range(NSA_HPG)], axis=0)
        slope = _alibi_rows(g * NSA_HPG, NSA_HPG, dec)
        dist_c = q_pos - (c * CMP_STRIDE + CMP_BLOCK - 1)
        s_c = _dot_nt(q, kc) - slope * dist_c.astype(F32)
        p_c = _masked_softmax(s_c, (dist_c >= 0) & (c < nc))
        oc_ref[0, g * rows:(g + 1) * rows, :] = _dot(p_c.astype(BF), vc)
        p_sum = p_c[0:dec]
        for h in range(1, NSA_HPG):
            p_sum = p_sum + p_c[h * dec:(h + 1) * dec]
        imp = _split_dot(p_sum, ovl_ref[...])
        cur = qp1 // SEL_BLOCK
        avail = jblk * SEL_BLOCK <= qp1
        forced = (jblk == 0) | (jblk == cur) | (jblk == cur - 1)
        score = jnp.where(avail, imp + FORCE_BONUS * forced.astype(F32), NEG_INF)
        sel = _select_topk(score, nsl)
        sel_ref[0, g * rows:(g + 1) * rows, :] = jnp.concatenate([sel] * NSA_HPG, axis=0)


def _overlap_matrix_wide(ncp, nc, nsl):
    c0 = np.arange(ncp)[:, None] * CMP_STRIDE
    c1 = c0 + CMP_BLOCK - 1
    s0 = np.arange(nsl)[None, :] * SEL_BLOCK
    s1 = s0 + SEL_BLOCK - 1
    ok = (c0 <= s1) & (c1 >= s0) & (np.arange(ncp)[:, None] < nc)
    return jnp.asarray(ok.astype(np.float32)).astype(BF)


def _nsa_sample_cmp(page_table, pool, qw, pe, w1, w2, past):
    db, n_pages = page_table.shape
    page = pool.shape[2]
    dec = qw.shape[1]
    nchunk = n_pages * page // CMP_STRIDE
    nc = (past + dec - CMP_BLOCK) // CMP_STRIDE + 1
    ns = -(-(past + dec) // SEL_BLOCK)
    nsl = -(-ns // LANES) * LANES
    ovl = _overlap_matrix_wide(nchunk, nc, nsl)
    rows = NSA_HEADS * dec
    pps = min(CMP_PAGES_PER_STEP, n_pages)
    assert n_pages % pps == 0
    cpp = page // CMP_STRIDE
    tok = np.arange(page)
    perm = np.zeros((page, page), np.float32)
    perm[tok, (tok % CMP_STRIDE) * cpp + tok // CMP_STRIDE] = 1.0
    perm = jnp.asarray(perm).astype(BF)
    full = lambda a: pl.BlockSpec(a.shape, lambda s, j, pt: (0,) * a.ndim)
    page_spec = lambda k: pl.BlockSpec((1, 2 * KVW, page), lambda s, j, pt: (pt[s, j * pps + k], 0, 0))
    return pl.pallas_call(
        functools.partial(_nsa_sample_cmp_kernel, pps=pps, past=past, nc=nc),
        grid_spec=pltpu.PrefetchScalarGridSpec(
            num_scalar_prefetch=1,
            grid=(db, n_pages // pps),
            in_specs=[pl.BlockSpec((1, dec, NSA_HEADS * HEAD_LANES), lambda s, j, pt: (s, 0, 0)),
                      full(pe), full(w1), full(w2), full(ovl), full(perm)] + [page_spec(k) for k in range(pps)],
            out_specs=[pl.BlockSpec((1, rows, KVW), lambda s, j, pt: (s, 0, 0)),
                       pl.BlockSpec((1, rows, nsl), lambda s, j, pt: (s, 0, 0))],
            scratch_shapes=[pltpu.VMEM((nchunk, 4 * NSA_DH), F32), pltpu.VMEM((nchunk + 8, 4 * NSA_DH), F32),
                            pltpu.VMEM((pps * page, KVW), F32), pltpu.VMEM((pps * page, KVW), F32)]),
        out_shape=[jax.ShapeDtypeStruct((db, rows, KVW), F32), jax.ShapeDtypeStruct((db, rows, nsl), F32)],
        compiler_params=pltpu.CompilerParams(dimension_semantics=("parallel", "arbitrary"),
                                             vmem_limit_bytes=56 * 1024 * 1024),
        name="nsa_sample_compress_select",
    )(page_table, qw, pe, w1, w2, ovl, perm, *([pool] * pps))


def _nsa_sample_sel_kernel(pt_ref, qw_ref, gt_ref, oc_ref, selt_ref, self_ref, new_ref, winold_ref, winnew_ref,
                           *rest, pps, past):
    pages = rest[:pps]
    o_ref, m_ref, l_ref, acc_ref = rest[pps:]
    step = pl.program_id(1)
    dec = qw_ref.shape[1]
    page = pages[0].shape[2]
    rows = NSA_HEADS * dec
    nk = pps * page

    @pl.when(step == 0)
    def _():
        _online_init(m_ref, l_ref, acc_ref)

    q = _stack_heads(qw_ref, NSA_HEADS, HEAD_LANES)
    slope = _alibi_rows(0, NSA_HEADS, dec)
    q_pos = past + lax.broadcasted_iota(jnp.int32, (rows, 1), 0) % dec
    kt = jnp.concatenate([pg[0, 0:KVW, :] for pg in pages], axis=1).astype(BF)
    vt = jnp.concatenate([pg[0, KVW:2 * KVW, :] for pg in pages], axis=1).astype(BF)
    key = lax.broadcasted_iota(jnp.int32, (1, nk), 1)
    kpos = step * nk + key
    blocks_per_step = nk // SEL_BLOCK
    tile_off = (step % (LANES // blocks_per_step)) * blocks_per_step
    expand = lax.broadcasted_iota(jnp.int32, (LANES, nk), 0) == tile_off + key // SEL_BLOCK
    picked = _dot(selt_ref[0].astype(BF), jnp.where(expand, 1.0, 0.0).astype(BF))
    dist = q_pos - kpos
    s = _dot(q, kt) - slope * dist.astype(F32)
    _online_update(jnp.where((picked > 0.5) & (dist >= 0), s, NEG_INF), vt, m_ref, l_ref, acc_ref,
                   v_transposed=True)

    @pl.when(step == pl.num_programs(1) - 1)
    def _():
        inew = lax.broadcasted_iota(jnp.int32, (1, LANES), 1)
        dist_n = q_pos - (past + inew)
        valid_n = (inew < dec) & (dist_n >= 0)
        new = _pad_rows(new_ref[0], LANES)
        blk_new = past // SEL_BLOCK
        picked_n = self_ref[0][:, blk_new:blk_new + 1]
        s_n = _dot_nt(q, new[:, 0:KVW].astype(BF)) - slope * dist_n.astype(F32)
        _online_update(jnp.where(valid_n & (picked_n > 0.5), s_n, NEG_INF), new[:, KVW:2 * KVW].astype(BF),
                       m_ref, l_ref, acc_ref)
        o_s = _online_result(l_ref, acc_ref)
        wb = winold_ref.shape[2]
        wpos = past - wb + lax.broadcasted_iota(jnp.int32, (1, wb), 1)
        dist_o = q_pos - wpos
        s_o = _dot(q, winold_ref[0, 0:KVW, :].astype(BF)) - slope * dist_o.astype(F32)
        s_o = jnp.where((dist_o >= 0) & (dist_o < WINDOW) & (wpos >= 0), s_o, NEG_INF)
        wnew = _pad_rows(winnew_ref[0], LANES)
        s_w = _dot_nt(q, wnew[:, 0:KVW].astype(BF)) - slope * dist_n.astype(F32)
        s_w = jnp.where(valid_n & (dist_n < WINDOW), s_w, NEG_INF)
        mw = jnp.maximum(jnp.max(s_o, axis=-1, keepdims=True), jnp.max(s_w, axis=-1, keepdims=True))
        mw = jnp.where(mw == NEG_INF, 0.0, mw)
        e_o = jnp.exp(s_o - mw)
        e_w = jnp.exp(s_w - mw)
        den = jnp.sum(e_o, axis=-1, keepdims=True) + jnp.sum(e_w, axis=-1, keepdims=True)
        o_w = (_dot_nt(e_o.astype(BF), winold_ref[0, KVW:2 * KVW, :].astype(BF))
               + _dot(e_w.astype(BF), wnew[:, KVW:2 * KVW].astype(BF))) / jnp.where(den > 0, den, 1.0)
        gates = jax.nn.sigmoid(gt_ref[0])
        o_c = oc_ref[0]
        for head in range(NSA_HEADS):
            r = slice(head * dec, (head + 1) * dec)
            out = (gates[:, 3 * head:3 * head + 1] * o_c[r] + gates[:, 3 * head + 1:3 * head + 2] * o_s[r]
                   + gates[:, 3 * head + 2:3 * head + 3] * o_w[r])
            o_ref[0, :, head * HEAD_LANES:(head + 1) * HEAD_LANES] = out.astype(BF)


def _nsa_sample_sel(page_table, pool, qw, gates, o_c, sel, new_rows, win_old, win_new, past):
    db, n_pages = page_table.shape
    page = pool.shape[2]
    dec = qw.shape[1]
    pps = min(SAMPLE_PAGES_PER_STEP, n_pages)
    assert n_pages % pps == 0
    rows = NSA_HEADS * dec
    nsl = sel.shape[2]
    steps_per_tile = LANES // (pps * page // SEL_BLOCK)
    blk3 = lambda a: pl.BlockSpec((1,) + a.shape[1:], lambda s, j, pt: (s, 0, 0))
    page_spec = lambda k: pl.BlockSpec((1, 2 * KVW, page), lambda s, j, pt: (pt[s, j * pps + k], 1, 0))
    return pl.pallas_call(
        functools.partial(_nsa_sample_sel_kernel, pps=pps, past=past),
        grid_spec=pltpu.PrefetchScalarGridSpec(
            num_scalar_prefetch=1,
            grid=(db, n_pages // pps),
            in_specs=[blk3(qw), blk3(gates), blk3(o_c),
                      pl.BlockSpec((1, rows, LANES), lambda s, j, pt: (s, 0, j // steps_per_tile)),
                      blk3(sel), blk3(new_rows), blk3(win_old), blk3(win_new)] + [page_spec(k) for k in range(pps)],
            out_specs=pl.BlockSpec((1, dec, NSA_HEADS * HEAD_LANES), lambda s, j, pt: (s, 0, 0)),
            scratch_shapes=[pltpu.VMEM((rows, 1), F32), pltpu.VMEM((rows, 1), F32), pltpu.VMEM((rows, KVW), F32)]),
        out_shape=jax.ShapeDtypeStruct((db, dec, NSA_HEADS * HEAD_LANES), BF),
        compiler_params=_cparams(("parallel", "arbitrary")),
        name="nsa_sample_select_window",
    )(page_table, qw, gates, o_c, sel, sel, new_rows, win_old, win_new, *([pool] * pps))


def _mla_sample_kernel(pt_ref, q_ref, new_ref, wuv_ref, *rest, pps):
    pages = rest[:pps]
    o_ref, m_ref, l_ref, acc_ref = rest[pps:]
    step = pl.program_id(1)
    dec = q_ref.shape[1]
    kvw = MLA_KV_RANK + MLA_ROPE

    @pl.when(step == 0)
    def _():
        _online_init(m_ref, l_ref, acc_ref)

    q = _stack_heads(q_ref, MLA_HEADS, MLA_LANES)[:, 0:kvw]
    kt = jnp.concatenate([pg[0] for pg in pages], axis=1).astype(BF)
    _online_update(_dot(q, kt) * MLA_SCALE, kt, m_ref, l_ref, acc_ref, v_transposed=True)

    @pl.when(step == pl.num_programs(1) - 1)
    def _():
        rows = MLA_HEADS * dec
        qi = lax.broadcasted_iota(jnp.int32, (rows, 1), 0) % dec
        inew = lax.broadcasted_iota(jnp.int32, (1, LANES), 1)
        new = _pad_rows(new_ref[0], LANES).astype(BF)
        s_n = jnp.where((inew < dec) & (inew <= qi), _dot_nt(q, new) * MLA_SCALE, NEG_INF)
        _online_update(s_n, new, m_ref, l_ref, acc_ref)
        o_lat = _online_result(l_ref, acc_ref)[:, 0:MLA_KV_RANK].astype(BF)
        out = _dot(o_lat[0:dec], wuv_ref[0])
        for h in range(1, MLA_HEADS):
            out = out + _dot(o_lat[h * dec:(h + 1) * dec], wuv_ref[h])
        o_ref[0] = out.astype(BF)


def _mla_sample(page_table, pool, qmla, new_rows, wuv):
    db, n_pages = page_table.shape
    kvw, page = pool.shape[1:]
    dec = qmla.shape[1]
    pps = min(SAMPLE_PAGES_PER_STEP, n_pages)
    assert n_pages % pps == 0
    rows = MLA_HEADS * dec
    blk3 = lambda a: pl.BlockSpec((1,) + a.shape[1:], lambda s, j, pt: (s, 0, 0))
    page_spec = lambda k: pl.BlockSpec((1, kvw, page), lambda s, j, pt: (pt[s, j * pps + k], 0, 0))
    return pl.pallas_call(
        functools.partial(_mla_sample_kernel, pps=pps),
        grid_spec=pltpu.PrefetchScalarGridSpec(
            num_scalar_prefetch=1,
            grid=(db, n_pages // pps),
            in_specs=[blk3(qmla), blk3(new_rows), pl.BlockSpec(wuv.shape, lambda s, j, pt: (0, 0, 0))]
                     + [page_spec(k) for k in range(pps)],
            out_specs=pl.BlockSpec((1, dec, MLA_HEADS * MLA_V), lambda s, j, pt: (s, 0, 0)),
            scratch_shapes=[pltpu.VMEM((rows, 1), F32), pltpu.VMEM((rows, 1), F32), pltpu.VMEM((rows, kvw), F32)]),
        out_shape=jax.ShapeDtypeStruct((db, dec, MLA_HEADS * MLA_V), BF),
        compiler_params=_cparams(("parallel", "arbitrary")),
        name="mla_sample",
    )(page_table, qmla, new_rows, wuv, *([pool] * pps))


def _tail_weights(w_in_parts, w_up_a, w_up_b, w_o, ln1_g, ln1_b, w_group, b_group, w_router, b_router,
                  exp_w_gate, exp_w_up, exp_w_down, ln2_g, ln2_b, w_pe, w_pg):
    w_ga, w_gb = w_in_parts
    wa, wr, br = _arrange_merge_weights(w_up_a, w_group, b_group, w_router, b_router)
    return dict(w_ga=w_ga, w_gb=w_gb, wa=wa, wb=w_up_b.astype(BF), wo=w_o.astype(BF), g1=ln1_g[None], b1=ln1_b[None],
                wr=wr, br=br, wg=exp_w_gate.astype(BF), wu=exp_w_up.astype(BF), wd=exp_w_down.astype(BF),
                g2=ln2_g[None], b2=ln2_b[None], wpg=w_pg.astype(BF), wpe=w_pe.astype(BF))


def kernel(x_prompt, x_sample, cache_nsa_kv, cache_mla, cache_nsa_win, page_table, p_prompt, p_sample,
           w_in, cmp_pe_k, cmp_pe_v, cmp_k_w1, cmp_k_w2, cmp_v_w1, cmp_v_w2,
           mla_q_norm, mla_kv_norm, mla_w_uq, mla_w_uk, mla_w_uv,
           w_up_a, w_up_b, w_o, ln1_g, ln1_b,
           w_group, b_group, w_router, b_router, exp_w_gate, exp_w_up, exp_w_down,
           ln2_g, ln2_b, w_pe, w_pg):
    depth = w_in.shape[0]
    b, s, d = x_prompt.shape
    db, dec, _ = x_sample.shape
    n_pool, page = cache_nsa_kv.shape[1:3]
    past = page_table.shape[1] * page
    wb = cache_nsa_win.shape[2]
    alpha = (2.0 * depth) ** 0.25
    np_, ns_ = b * s, db * dec
    tm_p, tm_s = min(256, np_), min(256, ns_)
    tab_p = _rope_tables(jnp.arange(s, dtype=jnp.int32))
    tab_s = jnp.tile(_rope_tables(past + jnp.arange(dec, dtype=jnp.int32)), (tm_s // dec, 1))
    page_table = page_table.astype(jnp.int32)
    hp = x_prompt.reshape(np_, d)
    hs = x_sample.reshape(ns_, d)
    nsa_p_l, nsa_s_l, mla_p_l, mla_s_l, win_p_l, win_s_l = [], [], [], [], [], []
    for l in range(depth):
        w1, w_ga, w_gb, wq2, wuk = _arrange_proj_weights(w_in[l], mla_w_uq[l], mla_w_uk[l])
        pe, cw1, cw2 = _arrange_compress_weights(cmp_pe_k[l], cmp_pe_v[l], cmp_k_w1[l], cmp_k_w2[l],
                                                 cmp_v_w1[l], cmp_v_w2[l])
        wuv = _arrange_wuv(mla_w_uv[l])
        tw = _tail_weights((w_ga, w_gb), w_up_a[l], w_up_b[l], w_o[l], ln1_g[l], ln1_b[l], w_group[l], b_group[l],
                           w_router[l], b_router[l], exp_w_gate[l], exp_w_up[l], exp_w_down[l],
                           ln2_g[l], ln2_b[l], w_pe[l], w_pg[l])
        qn, kvn = mla_q_norm[l][None], mla_kv_norm[l][None]
        qw, nsa, kcmp, vcmp, selbf, win, winbf, gt, qmla, mla, kmla = _project(hp, tab_p, s // tm_p, w1, qn, kvn, wq2,
                                                                                wuk, tm=tm_p)
        r3 = lambda a: a.reshape(b, s, -1)
        kcvc = _compress_prompt(r3(kcmp), r3(vcmp), pe, cw1, cw2)
        on = _nsa_prompt(r3(qw), r3(gt), kcvc, r3(selbf), r3(winbf)).reshape(np_, -1)
        om = _mla_prompt(r3(qmla), r3(kmla), wuv).reshape(np_, -1)
        hp = _layer_tail(hp, on, om, p_prompt[l].reshape(np_, -1), tw, alpha, tm=tm_p)
        nsa_p_l.append(nsa.reshape(b, s, 4, NSA_GROUPS, NSA_DH))
        mla_p_l.append(r3(mla))
        win_p_l.append(win.reshape(b, s, 2, NSA_GROUPS, NSA_DH)[:, -min(WINDOW, s):])
        qw, nsa, kcmp, vcmp, selbf, win, winbf, gt, qmla, mla, kmla = _project(hs, tab_s, 1, w1, qn, kvn, wq2, wuk,
                                                                                tm=tm_s)
        r3 = lambda a: a.reshape(db, dec, -1)
        pool_nsa = jnp.transpose(cache_nsa_kv[l], (0, 2, 3, 4, 1)).reshape(n_pool, 4 * KVW, page)
        pool_mla = jnp.transpose(cache_mla[l], (0, 2, 1))
        win_old = jnp.transpose(cache_nsa_win[l], (0, 2, 3, 4, 1)).reshape(db, 2 * KVW, wb)
        o_c, sel = _nsa_sample_cmp(page_table, pool_nsa, r3(qw), pe, cw1, cw2, past)
        on = _nsa_sample_sel(page_table, pool_nsa, r3(qw), r3(gt), o_c, sel, r3(nsa)[:, :, 2 * KVW:], win_old, r3(win),
                             past).reshape(ns_, -1)
        om = _mla_sample(page_table, pool_mla, r3(qmla), r3(mla), wuv).reshape(ns_, -1)
        hs = _layer_tail(hs, on, om, p_sample[l].reshape(ns_, -1), tw, alpha, tm=tm_s)
        nsa_s_l.append(nsa.reshape(db, dec, 4, NSA_GROUPS, NSA_DH))
        mla_s_l.append(r3(mla))
        win_new = win.reshape(db, dec, 2, NSA_GROUPS, NSA_DH)
        win_s_l.append(jnp.concatenate([cache_nsa_win[l], win_new], axis=1)[:, -wb:])
    return (hp.reshape(b, s, d), hs.reshape(db, dec, d), jnp.stack(nsa_p_l), jnp.stack(nsa_s_l),
            jnp.stack(mla_p_l), jnp.stack(mla_s_l), jnp.stack(win_p_l), jnp.stack(win_s_l))
```
